```python
import jax, jax.numpy as jnp
from jax import lax
import numpy as np

D_MODEL = 4096
BATCH = 1
SEQ = 8192
DEPTH = 4

N_A_LAYERS = DEPTH // 2
N_B_LAYERS = DEPTH - N_A_LAYERS
HG_HEAD_DIM = 128
HG_HEADS = D_MODEL // HG_HEAD_DIM
HG_CHUNK = 64
ATT_HEAD_DIM = 64
ATT_Q_HEADS = D_MODEL // ATT_HEAD_DIM
ATT_KV_HEADS = 8
ATT_GROUP = ATT_Q_HEADS // ATT_KV_HEADS
WINDOW = 128
ROPE_THETA = 10000.0
D_FF = 4 * D_MODEL
PLE_DIM = 256
NORM_EPS = 1e-6

kernel_name = "yoco_hgrn2_swa_sink_hybrid"

F32 = jnp.float32


def rms_norm(x, g):
    xf = x.astype(F32)
    y = xf * lax.rsqrt(jnp.mean(xf * xf, axis=-1, keepdims=True) + NORM_EPS)
    return (y * g.astype(F32)).astype(x.dtype)


def rope(x, positions):
    half = x.shape[-1] // 2
    inv_freq = ROPE_THETA ** (-jnp.arange(half, dtype=F32) / half)
    ang = positions.astype(F32)[..., None] * inv_freq
    cos = jnp.cos(ang)[:, :, None, :]
    sin = jnp.sin(ang)[:, :, None, :]
    x1 = x[..., :half].astype(F32)
    x2 = x[..., half:].astype(F32)
    return jnp.concatenate([x1 * cos - x2 * sin, x2 * cos + x1 * sin], axis=-1).astype(x.dtype)


def hgrn2_mixer(hn, w_in, w_out, lb, out_norm_g):
    B, T, _ = hn.shape
    nc = T // HG_CHUNK
    proj = hn @ w_in
    q, f_pre, i_in, g = jnp.split(proj, 4, axis=-1)
    f = lb + (1.0 - lb) * jax.nn.sigmoid(f_pre.astype(F32))
    log_f = jnp.log(f)
    k = 1.0 - f

    def to_chunks(a):
        a = a.astype(F32).reshape(B, nc, HG_CHUNK, HG_HEADS, HG_HEAD_DIM)
        return a.transpose(1, 0, 3, 2, 4)

    qc, kc, vc, gc = to_chunks(q), to_chunks(k), to_chunks(i_in), to_chunks(log_f)
    causal = jnp.tril(jnp.ones((HG_CHUNK, HG_CHUNK), dtype=bool))

    def step(S, inp):
        qb, kb, vb, gb = inp
        b = jnp.cumsum(gb, axis=-2)
        o_inter = jnp.einsum('bhtd,bhde->bhte', qb * jnp.exp(b), S)
        diff = b[:, :, :, None, :] - b[:, :, None, :, :]
        decay = jnp.exp(jnp.where(causal[:, :, None], diff, -jnp.inf))
        att = jnp.einsum('bhtd,bhsd,bhtsd->bhts', qb, kb, decay)
        o = o_inter + jnp.einsum('bhts,bhse->bhte', att, vb)
        b_last = b[:, :, -1:, :]
        S_new = jnp.exp(b_last[:, :, 0, :])[..., None] * S + jnp.einsum(
            'bhsd,bhse->bhde', kb * jnp.exp(b_last - b), vb)
        return S_new, o

    S0 = jnp.zeros((B, HG_HEADS, HG_HEAD_DIM, HG_HEAD_DIM), F32)
    _, o = lax.scan(step, S0, (qc, kc, vc, gc))
    o = o.transpose(1, 0, 3, 2, 4).reshape(B, T, HG_HEADS, HG_HEAD_DIM)
    o = o * lax.rsqrt(jnp.mean(o * o, axis=-1, keepdims=True) + NORM_EPS)
    o = o.reshape(B, T, D_MODEL) * out_norm_g.astype(F32) * jax.nn.silu(g.astype(F32))
    return o.astype(hn.dtype) @ w_out


def shared_kv(h, kv_norm_g, w_kv, positions):
    B, T, _ = h.shape
    kv = rms_norm(h, kv_norm_g) @ w_kv
    k, v = jnp.split(kv, 2, axis=-1)
    k = rope(k.reshape(B, T, ATT_KV_HEADS, ATT_HEAD_DIM), positions)
    v = v.reshape(B, T, ATT_KV_HEADS, ATT_HEAD_DIM)
    return k, v


def swa_sink_attention(q, k, v, sinks):
    B, T = q.shape[0], q.shape[1]
    nb = T // WINDOW
    qb = q.reshape(B, nb, WINDOW, ATT_KV_HEADS, ATT_GROUP, ATT_HEAD_DIM)

    def with_prev(a):
        a = a.reshape(B, nb, WINDOW, ATT_KV_HEADS, ATT_HEAD_DIM)
        prev = jnp.pad(a[:, :-1], ((0, 0), (1, 0), (0, 0), (0, 0), (0, 0)))
        return jnp.concatenate([prev, a], axis=2)

    kw, vw = with_prev(k), with_prev(v)
    scale = ATT_HEAD_DIM ** -0.5
    scores = jnp.einsum('bnqhgd,bnkhd->bhgnqk', qb, kw).astype(F32) * scale
    a_idx = jnp.arange(WINDOW)[:, None]
    c_idx = jnp.arange(2 * WINDOW)[None, :]
    band = (c_idx >= a_idx + 1) & (c_idx <= a_idx + WINDOW)
    blk = jnp.arange(nb)[:, None, None]
    valid = band[None] & ((blk > 0) | (c_idx[None] >= WINDOW))
    scores = jnp.where(valid, scores, -jnp.inf)
    sink = sinks.astype(F32).reshape(1, ATT_KV_HEADS, ATT_GROUP, 1, 1, 1)
    sink = jnp.broadcast_to(sink, scores.shape[:-1] + (1,))
    probs = jax.nn.softmax(jnp.concatenate([scores, sink], axis=-1), axis=-1)[..., :-1]
    out = jnp.einsum('bhgnqk,bnkhd->bnqhgd', probs.astype(v.dtype), vw)
    return out.reshape(B, T, ATT_Q_HEADS * ATT_HEAD_DIM)


def setup_inputs(seed: int = 0) -> dict:
    key = jax.random.key(seed)
    ks = iter(jax.random.split(key, 32))

    def nrm(shape, scale):
        return jax.random.normal(next(ks), shape, F32) * scale

    def gain(shape):
        return 1.0 + nrm(shape, 0.02)

    D = D_MODEL
    return {
        "x": nrm((BATCH, SEQ, D), 1.0),
        "p": nrm((DEPTH, BATCH, SEQ, PLE_DIM), 1.0),
        "positions": jnp.broadcast_to(jnp.arange(SEQ, dtype=jnp.int32), (BATCH, SEQ)),
        "mixer_norm_g": gain((DEPTH, D)),
        "hgrn_w_in": nrm((N_A_LAYERS, D, 4 * D), D ** -0.5),
        "hgrn_w_out": nrm((N_A_LAYERS, D, D), D ** -0.5),
        "hgrn_lb_logits": 1.0 + nrm((N_A_LAYERS, D), 0.5),
        "hgrn_out_norm_g": gain((N_A_LAYERS, D)),
        "kv_norm_g": gain((D,)),
        "w_kv": nrm((D, 2 * ATT_KV_HEADS * ATT_HEAD_DIM), D ** -0.5),
        "attn_w_q": nrm((N_B_LAYERS, D, ATT_Q_HEADS * ATT_HEAD_DIM), D ** -0.5),
        "attn_w_o": nrm((N_B_LAYERS, ATT_Q_HEADS * ATT_HEAD_DIM, D), (ATT_Q_HEADS * ATT_HEAD_DIM) ** -0.5),
        "attn_sinks": nrm((N_B_LAYERS, ATT_Q_HEADS), 0.5),
        "mlp_norm_g": gain((DEPTH, D)),
        "mlp_w1": nrm((DEPTH, D, D_FF), D ** -0.5),
        "mlp_w2": nrm((DEPTH, D_FF, D), D_FF ** -0.5),
        "ple_norm_g": gain((DEPTH, D)),
        "ple_w_gate": nrm((DEPTH, D, D), D ** -0.5),
        "ple_w_up": nrm((DEPTH, PLE_DIM, D), PLE_DIM ** -0.5),
        "final_norm_g": gain((D,)),
    }


def reference(x, p, positions, mixer_norm_g, hgrn_w_in, hgrn_w_out, hgrn_lb_logits,
              hgrn_out_norm_g, kv_norm_g, w_kv, attn_w_q, attn_w_o, attn_sinks,
              mlp_norm_g, mlp_w1, mlp_w2, ple_norm_g, ple_w_gate, ple_w_up, final_norm_g):
    B, T, _ = x.shape
    lb_soft = jax.nn.softmax(hgrn_lb_logits.astype(F32), axis=0)
    lower_bounds = jnp.cumsum(lb_soft, axis=0) - lb_soft[0]

    h = x
    k_sh = None
    v_sh = None
    for layer in range(DEPTH):
        hn = rms_norm(h, mixer_norm_g[layer])
        if layer < N_A_LAYERS:
            h = h + hgrn2_mixer(hn, hgrn_w_in[layer], hgrn_w_out[layer],
                                lower_bounds[layer], hgrn_out_norm_g[layer])
        else:
            j = layer - N_A_LAYERS
            q = (hn @ attn_w_q[j]).reshape(B, T, ATT_Q_HEADS, ATT_HEAD_DIM)
            q = rope(q, positions)
            h = h + swa_sink_attention(q, k_sh, v_sh, attn_sinks[j]) @ attn_w_o[j]
        hn = rms_norm(h, mlp_norm_g[layer])
        h = h + jnp.square(jax.nn.relu(hn @ mlp_w1[layer])) @ mlp_w2[layer]
        gate = jax.nn.sigmoid(rms_norm(h, ple_norm_g[layer]) @ ple_w_gate[layer])
        h = h + gate * (p[layer] @ ple_w_up[layer])
        if layer == N_A_LAYERS - 1:
            k_sh, v_sh = shared_kv(h, kv_norm_g, w_kv, positions)
    return rms_norm(h, final_norm_g)
```

```python
import functools

import jax
import jax.numpy as jnp
from jax import lax
from jax.experimental import pallas as pl
from jax.experimental.pallas import tpu as pltpu

F32 = jnp.float32
BF16 = jnp.bfloat16

LANES = 128
VMEM_LIMIT_BYTES = 56 * 1024 * 1024

HG_HEAD_DIM = 128
HG_CHUNK = 64
HG_SUB = 16
HG_SAFE_DECAY = 60.0
ATT_HEAD_DIM = 64
ATT_KV_HEADS = 8
WINDOW = 128
ROPE_THETA = 10000.0
NORM_EPS = 1e-6


def _cparams(*sem):
    return pltpu.CompilerParams(dimension_semantics=sem, vmem_limit_bytes=VMEM_LIMIT_BYTES)


def _rstd(ss, d_model):
    return lax.rsqrt(jnp.sum(ss, axis=-1, keepdims=True) * (1.0 / d_model) + NORM_EPS)


def _lane_partial_sumsq(x):
    sq = x * x
    acc = sq[:, 0:LANES]
    for c in range(1, x.shape[1] // LANES):
        acc = acc + sq[:, c * LANES:(c + 1) * LANES]
    return acc


def _emit_stream(h_new, g_ref, hn_ref, hb_ref, ss_ref):
    hn_ref[...] = h_new
    for i in range(hb_ref.shape[0]):
        hb_ref[i] = (h_new * g_ref[i:i + 1, :]).astype(BF16)
    ss_ref[...] = _lane_partial_sumsq(h_new)


def _swap_half_heads(x):
    lane = lax.broadcasted_iota(jnp.int32, x.shape, 1)
    first_half = (lane & (ATT_HEAD_DIM - 1)) < (ATT_HEAD_DIM // 2)
    return jnp.where(first_half, pltpu.roll(x, LANES - ATT_HEAD_DIM // 2, axis=1),
                     pltpu.roll(x, ATT_HEAD_DIM // 2, axis=1))


def _rope_tile(x, cos, sin_signed):
    outs = []
    for c in range(x.shape[1] // LANES):
        xc = x[:, c * LANES:(c + 1) * LANES]
        outs.append(xc * cos + _swap_half_heads(xc) * sin_signed)
    return outs


def _rope_table_kernel(pos_ref, invf_ref, sign_ref, cos_ref, sin_ref):
    ang = pos_ref[...].astype(F32) * invf_ref[...]
    cos_ref[...] = jnp.cos(ang)
    sin_ref[...] = jnp.sin(ang) * sign_ref[...]


def _rope_tables(positions):
    t = positions.shape[-1]
    half = ATT_HEAD_DIM // 2
    inv_freq = ROPE_THETA ** (-jnp.arange(half, dtype=F32) / half)
    invf = jnp.tile(inv_freq, LANES // half)[None, :]
    sign = jnp.tile(jnp.concatenate([-jnp.ones((half,), F32), jnp.ones((half,), F32)]),
                    LANES // ATT_HEAD_DIM)[None, :]
    bt = 1024
    return pl.pallas_call(
        _rope_table_kernel,
        grid=(t // bt,),
        in_specs=[pl.BlockSpec((bt, 1), lambda i: (i, 0)),
                  pl.BlockSpec((1, LANES), lambda i: (0, 0)),
                  pl.BlockSpec((1, LANES), lambda i: (0, 0))],
        out_specs=[pl.BlockSpec((bt, LANES), lambda i: (i, 0))] * 2,
        out_shape=[jax.ShapeDtypeStruct((t, LANES), F32)] * 2,
        compiler_params=_cparams("parallel"),
        name="rope_tables",
    )(positions.reshape(t, 1), invf, sign)


def _prep_kernel(x_ref, g_ref, hb_ref, ss_ref, *, ss_cols):
    x = x_ref[...]
    hb_ref[0] = (x * g_ref[...]).astype(BF16)
    for j in range(ss_ref.shape[1] // LANES):
        ss_ref[:, j * LANES:(j + 1) * LANES] = _lane_partial_sumsq(x[:, j * ss_cols:(j + 1) * ss_cols])


def _prep_stream(x, g):
    t, d = x.shape
    bt, ss_cols = 256, 1024
    ssw = LANES * (d // ss_cols)
    return pl.pallas_call(
        functools.partial(_prep_kernel, ss_cols=ss_cols),
        grid=(t // bt,),
        in_specs=[pl.BlockSpec((bt, d), lambda i: (i, 0)),
                  pl.BlockSpec((1, d), lambda i: (0, 0))],
        out_specs=[pl.BlockSpec((1, bt, d), lambda i: (0, i, 0)),
                   pl.BlockSpec((bt, ssw), lambda i: (i, 0))],
        out_shape=[jax.ShapeDtypeStruct((1, t, d), BF16),
                   jax.ShapeDtypeStruct((t, ssw), F32)],
        compiler_params=_cparams("parallel"),
        name="prep_stream",
    )(x, g.reshape(1, d))


def _final_norm_kernel(h_ref, ss_ref, g_ref, o_ref):
    d = h_ref.shape[1]
    o_ref[...] = h_ref[...] * _rstd(ss_ref[...], d) * g_ref[...]


def _final_norm(h, ss, g):
    t, d = h.shape
    bt = 256
    return pl.pallas_call(
        _final_norm_kernel,
        grid=(t // bt,),
        in_specs=[pl.BlockSpec((bt, d), lambda i: (i, 0)),
                  pl.BlockSpec((bt, ss.shape[1]), lambda i: (i, 0)),
                  pl.BlockSpec((1, d), lambda i: (0, 0))],
        out_specs=pl.BlockSpec((bt, d), lambda i: (i, 0)),
        out_shape=jax.ShapeDtypeStruct((t, d), F32),
        compiler_params=_cparams("parallel"),
        name="final_norm",
    )(h, ss, g.reshape(1, d))


def _proj_kernel(a_ref, ss_ref, w_ref, o_ref):
    acc = jnp.dot(a_ref[...], w_ref[...], preferred_element_type=F32)
    o_ref[...] = (acc * _rstd(ss_ref[...], a_ref.shape[1])).astype(o_ref.dtype)


def _proj_rope_kernel(a_ref, ss_ref, w_ref, cos_ref, sin_ref, o_ref):
    acc = jnp.dot(a_ref[...], w_ref[...], preferred_element_type=F32)
    acc = acc * _rstd(ss_ref[...], a_ref.shape[1])
    for c, piece in enumerate(_rope_tile(acc, cos_ref[...], sin_ref[...])):
        o_ref[:, c * LANES:(c + 1) * LANES] = piece.astype(o_ref.dtype)


def _kv_kernel(a_ref, ss_ref, w_ref, cos_ref, sin_ref, k_ref, v_ref):
    acc = jnp.dot(a_ref[...], w_ref[...], preferred_element_type=F32)
    acc = acc * _rstd(ss_ref[...], a_ref.shape[1])
    nk = k_ref.shape[1]
    for c, piece in enumerate(_rope_tile(acc[:, :nk], cos_ref[...], sin_ref[...])):
        k_ref[:, c * LANES:(c + 1) * LANES] = piece
    v_ref[...] = acc[:, nk:]


def _stream_specs(hb, ss, idx, bm):
    k = hb.shape[2]
    return [pl.BlockSpec((None, bm, k), lambda m, n: (idx, m, 0)),
            pl.BlockSpec((bm, ss.shape[1]), lambda m, n: (m, 0))]


def _proj(hb, ss, idx, w, out_dtype, name):
    _, t, k = hb.shape
    n = w.shape[1]
    bm, bn = 1024, 1024
    return pl.pallas_call(
        _proj_kernel,
        grid=(t // bm, n // bn),
        in_specs=_stream_specs(hb, ss, idx, bm) + [pl.BlockSpec((k, bn), lambda m, n: (0, n))],
        out_specs=pl.BlockSpec((bm, bn), lambda m, n: (m, n)),
        out_shape=jax.ShapeDtypeStruct((t, n), out_dtype),
        compiler_params=_cparams("parallel", "parallel"),
        name=name,
    )(hb, ss, w)


def _proj_rope(hb, ss, idx, w, cos, sin):
    _, t, k = hb.shape
    n = w.shape[1]
    bm, bn = 1024, 1024
    return pl.pallas_call(
        _proj_rope_kernel,
        grid=(t // bm, n // bn),
        in_specs=_stream_specs(hb, ss, idx, bm) + [
            pl.BlockSpec((k, bn), lambda m, n: (0, n)),
            pl.BlockSpec((bm, LANES), lambda m, n: (m, 0)),
            pl.BlockSpec((bm, LANES), lambda m, n: (m, 0))],
        out_specs=pl.BlockSpec((bm, bn), lambda m, n: (m, n)),
        out_shape=jax.ShapeDtypeStruct((t, n), BF16),
        compiler_params=_cparams("parallel", "parallel"),
        name="q_proj_rope",
    )(hb, ss, w, cos, sin)


def _kv_proj(hb, ss, idx, w, cos, sin):
    _, t, k = hb.shape
    n = w.shape[1]
    nk = n // 2
    bm = 1024
    return pl.pallas_call(
        _kv_kernel,
        grid=(t // bm, 1),
        in_specs=_stream_specs(hb, ss, idx, bm) + [
            pl.BlockSpec((k, n), lambda m, n: (0, 0)),
            pl.BlockSpec((bm, LANES), lambda m, n: (m, 0)),
            pl.BlockSpec((bm, LANES), lambda m, n: (m, 0))],
        out_specs=[pl.BlockSpec((bm, nk), lambda m, n: (m, 0))] * 2,
        out_shape=[jax.ShapeDtypeStruct((t, nk), F32)] * 2,
        compiler_params=_cparams("parallel", "arbitrary"),
        name="kv_proj_rope",
    )(hb, ss, w, cos, sin)


def _resid_kernel(a_ref, w_ref, h_ref, g_ref, hn_ref, hb_ref, ss_ref):
    acc = jnp.dot(a_ref[...], w_ref[...], preferred_element_type=F32)
    _emit_stream(h_ref[...] + acc, g_ref, hn_ref, hb_ref, ss_ref)


def _ple_kernel(a_ref, ssin_ref, w_ref, p_ref, wup_ref, h_ref, g_ref, hn_ref, hb_ref, ss_ref):
    acc = jnp.dot(a_ref[...], w_ref[...], preferred_element_type=F32)
    z = acc * _rstd(ssin_ref[...], a_ref.shape[1])
    gate = 1.0 / (1.0 + jnp.exp(-z))
    up = jnp.dot(p_ref[...].astype(BF16), wup_ref[...], preferred_element_type=F32)
    _emit_stream(h_ref[...] + gate * up, g_ref, hn_ref, hb_ref, ss_ref)


def _stream_out(t, d, ng, bm, bn):
    specs = [pl.BlockSpec((bm, bn), lambda m, n: (m, n)),
             pl.BlockSpec((ng, bm, bn), lambda m, n: (0, m, n)),
             pl.BlockSpec((bm, LANES), lambda m, n: (m, n))]
    shapes = [jax.ShapeDtypeStruct((t, d), F32),
              jax.ShapeDtypeStruct((ng, t, d), BF16),
              jax.ShapeDtypeStruct((t, LANES * (d // bn)), F32)]
    return specs, shapes


def _resid_matmul(a, w, h, gains, name):
    t, k = a.shape
    d = w.shape[1]
    ng = gains.shape[0]
    bm, bn = 512, 1024
    out_specs, out_shape = _stream_out(t, d, ng, bm, bn)
    return pl.pallas_call(
        _resid_kernel,
        grid=(t // bm, d // bn),
        in_specs=[pl.BlockSpec((bm, k), lambda m, n: (m, 0)),
                  pl.BlockSpec((k, bn), lambda m, n: (0, n)),
                  pl.BlockSpec((bm, bn), lambda m, n: (m, n)),
                  pl.BlockSpec((ng, bn), lambda m, n: (0, n))],
        out_specs=out_specs,
        out_shape=out_shape,
        compiler_params=_cparams("parallel", "parallel"),
        name=name,
    )(a, w, h, gains)


def _ple(hb, ss, w_gate, p, w_up, h, gains):
    _, t, k = hb.shape
    d = w_gate.shape[1]
    ng = gains.shape[0]
    pdim = p.shape[1]
    bm, bn = 512, 1024
    out_specs, out_shape = _stream_out(t, d, ng, bm, bn)
    return pl.pallas_call(
        _ple_kernel,
        grid=(t // bm, d // bn),
        in_specs=_stream_specs(hb, ss, 0, bm) + [
            pl.BlockSpec((k, bn), lambda m, n: (0, n)),
            pl.BlockSpec((bm, pdim), lambda m, n: (m, 0)),
            pl.BlockSpec((pdim, bn), lambda m, n: (0, n)),
            pl.BlockSpec((bm, bn), lambda m, n: (m, n)),
            pl.BlockSpec((ng, bn), lambda m, n: (0, n))],
        out_specs=out_specs,
        out_shape=out_shape,
        compiler_params=_cparams("parallel", "parallel"),
        name="ple",
    )(hb, ss, w_gate, p, w_up, h, gains)


def _mlp_kernel(a_ref, ssin_ref, w1_ref, w2_ref, h_ref, g_ref, hn_ref, hb_ref, ss_ref, acc_ref,
                *, n_ff, ec):
    f = pl.program_id(1)

    @pl.when(f < n_ff)
    def _():
        z = jnp.dot(a_ref[...], w1_ref[...], preferred_element_type=F32)
        z = jnp.maximum(z * _rstd(ssin_ref[...], a_ref.shape[1]), 0.0)
        act = (z * z).astype(BF16)
        for j in range(acc_ref.shape[0]):
            part = jnp.dot(act, w2_ref[:, j * ec:(j + 1) * ec], preferred_element_type=F32)

            @pl.when(f == 0)
            def _():
                acc_ref[j] = part

            @pl.when(f > 0)
            def _():
                acc_ref[j] += part

    @pl.when(f >= n_ff)
    def _():
        _emit_stream(h_ref[...] + acc_ref[f - n_ff], g_ref, hn_ref, hb_ref, ss_ref)


def _mlp(hb, ss, w1, w2, h, gains):
    _, t, k = hb.shape
    d_ff = w1.shape[1]
    d = w2.shape[1]
    ng = gains.shape[0]
    bm, bf, ec = 512, 512, 1024
    n_ff, n_ec = d_ff // bf, d // ec
    last = n_ff - 1

    def epi(m, f):
        return (m, jnp.maximum(f - n_ff, 0))

    return pl.pallas_call(
        functools.partial(_mlp_kernel, n_ff=n_ff, ec=ec),
        grid=(t // bm, n_ff + n_ec),
        in_specs=_stream_specs(hb, ss, 0, bm) + [
            pl.BlockSpec((k, bf), lambda m, f: (0, jnp.minimum(f, last))),
            pl.BlockSpec((bf, d), lambda m, f: (jnp.minimum(f, last), 0)),
            pl.BlockSpec((bm, ec), epi),
            pl.BlockSpec((ng, ec), lambda m, f: (0, jnp.maximum(f - n_ff, 0)))],
        out_specs=[pl.BlockSpec((bm, ec), epi),
                   pl.BlockSpec((ng, bm, ec), lambda m, f: (0, m, jnp.maximum(f - n_ff, 0))),
                   pl.BlockSpec((bm, LANES), epi)],
        out_shape=[jax.ShapeDtypeStruct((t, d), F32),
                   jax.ShapeDtypeStruct((ng, t, d), BF16),
                   jax.ShapeDtypeStruct((t, LANES * n_ec), F32)],
        scratch_shapes=[pltpu.VMEM((n_ec, bm, ec), F32)],
        compiler_params=_cparams("parallel", "arbitrary"),
        name="mlp",
    )(hb, ss, w1, w2, h, gains)


def _hgrn_kernel(q_ref, f_ref, i_ref, g_ref, lbl_ref, ong_ref, o_ref, st_ref, b_ref, k_ref,
                 *, layer, unroll):
    bt = q_ref.shape[0]
    c_len, sb = HG_CHUNK, HG_SUB
    n_sb = c_len // sb
    dn_last = (((1,), (1,)), ((), ()))

    @pl.when(pl.program_id(1) == 0)
    def _():
        st_ref[...] = jnp.zeros_like(st_ref)

    lg = lbl_ref[...]
    e = jnp.exp(lg - jnp.max(lg, axis=0, keepdims=True))
    sm = e / jnp.sum(e, axis=0, keepdims=True)
    cs = sm[0:1]
    for l in range(1, layer + 1):
        cs = cs + sm[l:l + 1]
    lb = cs - sm[0:1]

    sig = 1.0 / (1.0 + jnp.exp(-f_ref[...]))
    fgate = lb + (1.0 - lb) * sig
    k_ref[...] = 1.0 - fgate
    row = lax.broadcasted_iota(jnp.int32, (bt, HG_HEAD_DIM), 0) & (c_len - 1)
    b = jnp.log(fgate)
    shift = 1
    while shift < c_len:
        b = b + jnp.where(row >= shift, pltpu.roll(b, shift, axis=0), 0.0)
        shift *= 2
    b_ref[...] = b

    ends = b_ref[pl.ds(sb - 1, bt // sb, stride=sb), :]
    er = lax.broadcasted_iota(jnp.int32, ends.shape, 0) & (n_sb - 1)
    prev = jnp.where(er == 0, 0.0, pltpu.roll(ends, 1, axis=0))
    safe = jnp.max(prev - ends) <= HG_SAFE_DECAY

    tri = (lax.broadcasted_iota(jnp.int32, (c_len, c_len), 1)
           <= lax.broadcasted_iota(jnp.int32, (c_len, c_len), 0))
    rowc = lax.broadcasted_iota(jnp.int32, (c_len, 1), 0)
    ong = ong_ref[...]

    def chunk(c, fast):
        r0 = pl.multiple_of(c * c_len, c_len)
        sl = pl.ds(r0, c_len)
        q, v, gg, bc, kk = q_ref[sl, :], i_ref[sl, :], g_ref[sl, :], b_ref[sl, :], k_ref[sl, :]
        st = st_ref[...]
        bl = bc[c_len - 1:c_len, :]
        vb = v.astype(BF16)

        o = lax.dot_general((q * jnp.exp(bc)).astype(BF16), st.astype(BF16), dn_last,
                            preferred_element_type=F32)

        if fast:
            refs = [jnp.zeros((1, HG_HEAD_DIM), F32)] + [bc[sb * i - 1:sb * i, :] for i in range(1, n_sb)]
            w = jnp.concatenate([bc[sb * i:sb * (i + 1), :] - refs[i] for i in range(n_sb)], axis=0)
            qe = (q * jnp.exp(w)).astype(BF16)
            kb = kk * jnp.exp(-w)
            att_rows = []
            for i in range(n_sb):
                parts = []
                for j in range(n_sb):
                    kj = kb[sb * j:sb * (j + 1), :]
                    if j < i:
                        parts.append(kj * jnp.exp(refs[i] - refs[j]))
                    elif j == i:
                        parts.append(kj)
                    else:
                        parts.append(jnp.zeros_like(kj))
                ki = jnp.concatenate(parts, axis=0).astype(BF16)
                att_rows.append(lax.dot_general(qe[sb * i:sb * (i + 1), :], ki, dn_last,
                                                preferred_element_type=F32))
            att = jnp.where(tri, jnp.concatenate(att_rows, axis=0), 0.0)
            o = o + jnp.dot(att.astype(BF16), vb, preferred_element_type=F32)
        else:
            def key_row(s, acc):
                bs = b_ref[pl.ds(r0 + s, 1), :]
                ks = k_ref[pl.ds(r0 + s, 1), :]
                vs = i_ref[pl.ds(r0 + s, 1), :]
                a = jnp.sum(q * ks * jnp.exp(jnp.minimum(bc - bs, 0.0)), axis=-1, keepdims=True)
                return acc + jnp.where(rowc >= s, a, 0.0) * vs

            o = o + lax.fori_loop(0, c_len, key_row, jnp.zeros((c_len, HG_HEAD_DIM), F32))

        kdec = (kk * jnp.exp(bl - bc)).astype(BF16)
        st_ref[...] = st * jnp.exp(bl) + jnp.dot(v.T.astype(BF16), kdec, preferred_element_type=F32)

        on = o * lax.rsqrt(jnp.mean(o * o, axis=-1, keepdims=True) + NORM_EPS)
        silu = gg / (1.0 + jnp.exp(-gg))
        o_ref[sl, :] = (on * ong * silu).astype(BF16)

    n_chunks = bt // c_len

    @pl.when(safe)
    def _():
        def body(c, carry):
            chunk(c, True)
            return carry
        lax.fori_loop(0, n_chunks, body, 0, unroll=unroll)

    @pl.when(jnp.logical_not(safe))
    def _():
        def body(c, carry):
            chunk(c, False)
            return carry
        lax.fori_loop(0, n_chunks, body, 0)


def _hgrn_scan(proj, lb_logits, out_norm_g, layer):
    t = proj.shape[0]
    d = proj.shape[1] // 4
    heads = d // HG_HEAD_DIM
    n_a = lb_logits.shape[0]
    bt = 1024

    def col(part):
        return pl.BlockSpec((bt, HG_HEAD_DIM), lambda h, i: (i, part * heads + h))

    return pl.pallas_call(
        functools.partial(_hgrn_kernel, layer=layer, unroll=2),
        grid=(heads, t // bt),
        in_specs=[col(0), col(1), col(2), col(3),
                  pl.BlockSpec((n_a, HG_HEAD_DIM), lambda h, i: (0, h)),
                  pl.BlockSpec((1, HG_HEAD_DIM), lambda h, i: (0, h))],
        out_specs=pl.BlockSpec((bt, HG_HEAD_DIM), lambda h, i: (i, h)),
        out_shape=jax.ShapeDtypeStruct((t, d), BF16),
        scratch_shapes=[pltpu.VMEM((HG_HEAD_DIM, HG_HEAD_DIM), F32),
                        pltpu.VMEM((bt, HG_HEAD_DIM), F32),
                        pltpu.VMEM((bt, HG_HEAD_DIM), F32)],
        compiler_params=_cparams("parallel", "arbitrary"),
        name="hgrn_scan",
    )(proj, proj, proj, proj, lb_logits, out_norm_g.reshape(1, d))


def _attn_kernel(sink_ref, q_ref, kc_ref, kp_ref, vc_ref, vp_ref, o_ref):
    n, j = pl.program_id(0), pl.program_id(1)
    group = q_ref.shape[1] // ATT_HEAD_DIM
    n_slab = q_ref.shape[1] // LANES
    dn_last = (((1,), (1,)), ((), ()))

    k2 = jnp.concatenate([kp_ref[...], kc_ref[...]], axis=0)
    v2 = jnp.concatenate([vp_ref[...], vc_ref[...]], axis=0)
    lane = lax.broadcasted_iota(jnp.int32, k2.shape, 1)
    kv_low = (j % 2) == 0
    k2r, v2r = pltpu.roll(k2, ATT_HEAD_DIM, axis=1), pltpu.roll(v2, ATT_HEAD_DIM, axis=1)
    low = lane < ATT_HEAD_DIM
    k_a = jnp.where(low, jnp.where(kv_low, k2, k2r), 0.0).astype(BF16)
    k_b = jnp.where(low, 0.0, jnp.where(kv_low, k2r, k2)).astype(BF16)
    v_a = jnp.where(low, jnp.where(kv_low, v2, v2r), 0.0).astype(BF16)
    v_b = jnp.where(low, 0.0, jnp.where(kv_low, v2r, v2)).astype(BF16)

    qs = jnp.concatenate([q_ref[:, p * LANES:(p + 1) * LANES] for p in range(n_slab)], axis=0)
    rows = qs.shape[0]
    a_idx = lax.broadcasted_iota(jnp.int32, (rows, 2 * WINDOW), 0) & (WINDOW - 1)
    c_idx = lax.broadcasted_iota(jnp.int32, (rows, 2 * WINDOW), 1)
    valid = (c_idx >= a_idx + 1) & (c_idx <= a_idx + WINDOW) & ((n > 0) | (c_idx >= WINDOW))
    scale = ATT_HEAD_DIM ** -0.5

    def probs(k_half, parity):
        s = lax.dot_general(qs, k_half, dn_last, preferred_element_type=F32) * scale
        s = jnp.where(valid, s, -jnp.inf)
        sink = jnp.concatenate(
            [jnp.full((WINDOW, 1), sink_ref[group * j + 2 * p + parity], F32) for p in range(n_slab)],
            axis=0)
        m = jnp.maximum(jnp.max(s, axis=-1, keepdims=True), sink)
        e = jnp.exp(s - m)
        den = jnp.sum(e, axis=-1, keepdims=True) + jnp.exp(sink - m)
        return (e / den).astype(BF16)

    out = (jnp.dot(probs(k_a, 0), v_a, preferred_element_type=F32)
           + jnp.dot(probs(k_b, 1), v_b, preferred_element_type=F32))
    for p in range(n_slab):
        o_ref[:, p * LANES:(p + 1) * LANES] = out[p * WINDOW:(p + 1) * WINDOW, :].astype(BF16)


def _attention(q, k, v, sinks):
    t, dq = q.shape
    nb = t // WINDOW
    gw = dq // ATT_KV_HEADS

    def cur(n, j):
        return (n, j // 2)

    def prev(n, j):
        return (jnp.maximum(n - 1, 0), j // 2)

    return pl.pallas_call(
        _attn_kernel,
        grid=(nb, ATT_KV_HEADS),
        in_specs=[pl.BlockSpec(memory_space=pltpu.SMEM),
                  pl.BlockSpec((WINDOW, gw), lambda n, j: (n, j)),
                  pl.BlockSpec((WINDOW, LANES), cur),
                  pl.BlockSpec((WINDOW, LANES), prev),
                  pl.BlockSpec((WINDOW, LANES), cur),
                  pl.BlockSpec((WINDOW, LANES), prev)],
        out_specs=pl.BlockSpec((WINDOW, gw), lambda n, j: (n, j)),
        out_shape=jax.ShapeDtypeStruct((t, dq), BF16),
        compiler_params=_cparams("parallel", "parallel"),
        name="swa_attention",
    )(sinks, q, k, k, v, v)


def kernel(x, p, positions, mixer_norm_g, hgrn_w_in, hgrn_w_out, hgrn_lb_logits, hgrn_out_norm_g,
           kv_norm_g, w_kv, attn_w_q, attn_w_o, attn_sinks, mlp_norm_g, mlp_w1, mlp_w2,
           ple_norm_g, ple_w_gate, ple_w_up, final_norm_g):
    batch, t, d = x.shape
    depth = mlp_w1.shape[0]
    n_a = hgrn_w_in.shape[0]
    assert batch == 1

    cos, sin = _rope_tables(positions)
    h = x.reshape(t, d)
    hb, ss = _prep_stream(h, mixer_norm_g[0])
    k_sh = v_sh = None

    for layer in range(depth):
        mlp_gain = mlp_norm_g[layer][None, :]
        if layer < n_a:
            proj = _proj(hb, ss, 0, hgrn_w_in[layer].astype(BF16), F32, "hgrn_in_proj")
            mixed = _hgrn_scan(proj, hgrn_lb_logits, hgrn_out_norm_g[layer], layer)
            h, hb, ss = _resid_matmul(mixed, hgrn_w_out[layer].astype(BF16), h, mlp_gain, "hgrn_out_proj")
        else:
            j = layer - n_a
            q = _proj_rope(hb, ss, 0, attn_w_q[j].astype(BF16), cos, sin)
            mixed = _attention(q, k_sh, v_sh, attn_sinks[j])
            h, hb, ss = _resid_matmul(mixed, attn_w_o[j].astype(BF16), h, mlp_gain, "attn_out_proj")

        h, hb, ss = _mlp(hb, ss, mlp_w1[layer].astype(BF16), mlp_w2[layer].astype(BF16), h,
                         ple_norm_g[layer][None, :])

        if layer + 1 < depth:
            next_gains = [mixer_norm_g[layer + 1]]
            if layer == n_a - 1:
                next_gains.append(kv_norm_g)
        else:
            next_gains = [final_norm_g]
        h, hb, ss = _ple(hb, ss, ple_w_gate[layer].astype(BF16), p[layer, 0],
                         ple_w_up[layer].astype(BF16), h, jnp.stack(next_gains))

        if layer == n_a - 1:
            k_sh, v_sh = _kv_proj(hb, ss, 1, w_kv.astype(BF16), cos, sin)

    return _final_norm(h, ss, final_norm_g).reshape(batch, t, d)
```

```python
import functools

import jax
import jax.numpy as jnp
from jax import lax
from jax.experimental import pallas as pl
from jax.experimental.pallas import tpu as pltpu

F32 = jnp.float32
BF16 = jnp.bfloat16

LANES = 128
MXU_COLS = 256
VMEM_LIMIT_BYTES = 56 * 1024 * 1024

HG_HEAD_DIM = 128
HG_CHUNK = 64
HG_SUB = 16
HG_SAFE_DECAY = 60.0
ATT_HEAD_DIM = 64
ATT_KV_HEADS = 8
WINDOW = 128
ROPE_THETA = 10000.0
NORM_EPS = 1e-6

DN_LAST = (((1,), (1,)), ((), ()))


def _cparams(*sem):
    return pltpu.CompilerParams(dimension_semantics=sem, vmem_limit_bytes=VMEM_LIMIT_BYTES)


def _rstd(ss_ref, d_model):
    return lax.rsqrt(jnp.sum(ss_ref[...], axis=-1, keepdims=True) * (1.0 / d_model) + NORM_EPS)


def _lane_partial_sumsq(x):
    sq = x * x
    acc = sq[:, 0:LANES]
    for c in range(1, x.shape[1] // LANES):
        acc = acc + sq[:, c * LANES:(c + 1) * LANES]
    return acc


def _col_chunks(width):
    return [slice(c, c + MXU_COLS) for c in range(0, width, MXU_COLS)]


def _emit_cols(h_new, cs, g_ref, hn_ref, hb_ref):
    hn_ref[:, cs] = h_new
    for i in range(hb_ref.shape[0]):
        hb_ref[i, :, cs] = (h_new * g_ref[i:i + 1, cs]).astype(BF16)
    return _lane_partial_sumsq(h_new)


def _accumulate_ss(ss_ref, part, first):
    @pl.when(first)
    def _():
        ss_ref[...] = jnp.zeros_like(ss_ref)
    ss_ref[...] += part


def _swap_half_heads(x):
    lane = lax.broadcasted_iota(jnp.int32, x.shape, 1)
    first_half = (lane & (ATT_HEAD_DIM - 1)) < (ATT_HEAD_DIM // 2)
    return jnp.where(first_half, pltpu.roll(x, LANES - ATT_HEAD_DIM // 2, axis=1),
                     pltpu.roll(x, ATT_HEAD_DIM // 2, axis=1))


def _rope_store(x, cos, sin_signed, o_ref, col0):
    for c in range(x.shape[1] // LANES):
        xc = x[:, c * LANES:(c + 1) * LANES]
        piece = xc * cos + _swap_half_heads(xc) * sin_signed
        o_ref[:, col0 + c * LANES:col0 + (c + 1) * LANES] = piece.astype(o_ref.dtype)


def _rope_table_kernel(pos_ref, invf_ref, sign_ref, cos_ref, sin_ref):
    ang = pos_ref[...].astype(F32) * invf_ref[...]
    cos_ref[...] = jnp.cos(ang)
    sin_ref[...] = jnp.sin(ang) * sign_ref[...]


def _rope_tables(positions):
    t = positions.shape[-1]
    half = ATT_HEAD_DIM // 2
    inv_freq = ROPE_THETA ** (-jnp.arange(half, dtype=F32) / half)
    invf = jnp.tile(inv_freq, LANES // half)[None, :]
    sign = jnp.tile(jnp.concatenate([-jnp.ones((half,), F32), jnp.ones((half,), F32)]),
                    LANES // ATT_HEAD_DIM)[None, :]
    bt = 1024
    return pl.pallas_call(
        _rope_table_kernel,
        grid=(t // bt,),
        in_specs=[pl.BlockSpec((bt, 1), lambda i: (i, 0)),
                  pl.BlockSpec((1, LANES), lambda i: (0, 0)),
                  pl.BlockSpec((1, LANES), lambda i: (0, 0))],
        out_specs=[pl.BlockSpec((bt, LANES), lambda i: (i, 0))] * 2,
        out_shape=[jax.ShapeDtypeStruct((t, LANES), F32)] * 2,
        compiler_params=_cparams("parallel"),
        name="rope_tables",
    )(positions.reshape(t, 1), invf, sign)


def _prep_kernel(x_ref, g_ref, hb_ref, ss_ref):
    x = x_ref[...]
    hb_ref[0] = (x * g_ref[...]).astype(BF16)
    ss_ref[...] = _lane_partial_sumsq(x)


def _prep_stream(x, g):
    t, d = x.shape
    bt = 256
    return pl.pallas_call(
        _prep_kernel,
        grid=(t // bt,),
        in_specs=[pl.BlockSpec((bt, d), lambda i: (i, 0)),
                  pl.BlockSpec((1, d), lambda i: (0, 0))],
        out_specs=[pl.BlockSpec((1, bt, d), lambda i: (0, i, 0)),
                   pl.BlockSpec((bt, LANES), lambda i: (i, 0))],
        out_shape=[jax.ShapeDtypeStruct((1, t, d), BF16),
                   jax.ShapeDtypeStruct((t, LANES), F32)],
        compiler_params=_cparams("parallel"),
        name="prep_stream",
    )(x, g.reshape(1, d))


def _final_norm_kernel(h_ref, ss_ref, g_ref, o_ref):
    o_ref[...] = h_ref[...] * _rstd(ss_ref, h_ref.shape[1]) * g_ref[...]


def _final_norm(h, ss, g):
    t, d = h.shape
    bt = 256
    return pl.pallas_call(
        _final_norm_kernel,
        grid=(t // bt,),
        in_specs=[pl.BlockSpec((bt, d), lambda i: (i, 0)),
                  pl.BlockSpec((bt, LANES), lambda i: (i, 0)),
                  pl.BlockSpec((1, d), lambda i: (0, 0))],
        out_specs=pl.BlockSpec((bt, d), lambda i: (i, 0)),
        out_shape=jax.ShapeDtypeStruct((t, d), F32),
        compiler_params=_cparams("parallel"),
        name="final_norm",
    )(h, ss, g.reshape(1, d))


def _proj_kernel(a_ref, ss_ref, w_ref, o_ref):
    acc = jnp.dot(a_ref[...], w_ref[...], preferred_element_type=F32)
    o_ref[...] = (acc * _rstd(ss_ref, a_ref.shape[1])).astype(o_ref.dtype)


def _proj_rope_kernel(a_ref, ss_ref, w_ref, cos_ref, sin_ref, o_ref, *, scale):
    row_scale = _rstd(ss_ref, a_ref.shape[1]) * scale
    cos, sin = cos_ref[...], sin_ref[...]
    for cs in _col_chunks(o_ref.shape[1]):
        acc = jnp.dot(a_ref[...], w_ref[:, cs], preferred_element_type=F32)
        _rope_store(acc * row_scale, cos, sin, o_ref, cs.start)


def _kv_kernel(a_ref, ss_ref, w_ref, cos_ref, sin_ref, k_ref, v_ref):
    rstd = _rstd(ss_ref, a_ref.shape[1])
    nk = k_ref.shape[1]
    cos, sin = cos_ref[...], sin_ref[...]
    for cs in _col_chunks(nk):
        acc = jnp.dot(a_ref[...], w_ref[:, cs], preferred_element_type=F32)
        _rope_store(acc * rstd, cos, sin, k_ref, cs.start)
    for cs in _col_chunks(nk):
        acc = jnp.dot(a_ref[...], w_ref[:, nk + cs.start:nk + cs.stop], preferred_element_type=F32)
        v_ref[:, cs] = acc * rstd


def _stream_specs(hb, idx, bm):
    k = hb.shape[2]
    return [pl.BlockSpec((None, bm, k), lambda m, n: (idx, m, 0)),
            pl.BlockSpec((bm, LANES), lambda m, n: (m, 0))]


def _weight_spec(w, layer, bn):
    return pl.BlockSpec((None, w.shape[1], bn), lambda m, n: (layer, 0, n))


def _proj(hb, ss, idx, w, layer, out_dtype, name):
    _, t, _ = hb.shape
    n = w.shape[2]
    bm, bn = 1024, 1024
    return pl.pallas_call(
        _proj_kernel,
        grid=(t // bm, n // bn),
        in_specs=_stream_specs(hb, idx, bm) + [_weight_spec(w, layer, bn)],
        out_specs=pl.BlockSpec((bm, bn), lambda m, n: (m, n)),
        out_shape=jax.ShapeDtypeStruct((t, n), out_dtype),
        compiler_params=_cparams("parallel", "parallel"),
        name=name,
    )(hb, ss, w)


def _proj_rope(hb, ss, idx, w, layer, cos, sin, scale):
    _, t, _ = hb.shape
    n = w.shape[2]
    bm, bn = 1024, 1024
    return pl.pallas_call(
        functools.partial(_proj_rope_kernel, scale=scale),
        grid=(t // bm, n // bn),
        in_specs=_stream_specs(hb, idx, bm) + [
            _weight_spec(w, layer, bn),
            pl.BlockSpec((bm, LANES), lambda m, n: (m, 0)),
            pl.BlockSpec((bm, LANES), lambda m, n: (m, 0))],
        out_specs=pl.BlockSpec((bm, bn), lambda m, n: (m, n)),
        out_shape=jax.ShapeDtypeStruct((t, n), BF16),
        compiler_params=_cparams("parallel", "parallel"),
        name="q_proj_rope",
    )(hb, ss, w, cos, sin)


def _kv_proj(hb, ss, idx, w, cos, sin):
    _, t, _ = hb.shape
    n = w.shape[2]
    nk = n // 2
    bm = 1024
    return pl.pallas_call(
        _kv_kernel,
        grid=(t // bm, 1),
        in_specs=_stream_specs(hb, idx, bm) + [
            _weight_spec(w, 0, n),
            pl.BlockSpec((bm, LANES), lambda m, n: (m, 0)),
            pl.BlockSpec((bm, LANES), lambda m, n: (m, 0))],
        out_specs=[pl.BlockSpec((bm, nk), lambda m, n: (m, 0))] * 2,
        out_shape=[jax.ShapeDtypeStruct((t, nk), F32)] * 2,
        compiler_params=_cparams("parallel", "arbitrary"),
        name="kv_proj_rope",
    )(hb, ss, w, cos, sin)


def _resid_kernel(a_ref, w_ref, h_ref, g_ref, hn_ref, hb_ref, ss_ref):
    part = None
    for cs in _col_chunks(hn_ref.shape[1]):
        acc = jnp.dot(a_ref[...], w_ref[:, cs], preferred_element_type=F32)
        p = _emit_cols(h_ref[:, cs] + acc, cs, g_ref, hn_ref, hb_ref)
        part = p if part is None else part + p
    _accumulate_ss(ss_ref, part, pl.program_id(1) == 0)


def _ple_kernel(a_ref, ssin_ref, w_ref, p_ref, wup_ref, h_ref, g_ref, hn_ref, hb_ref, ss_ref):
    rstd = _rstd(ssin_ref, a_ref.shape[1])
    pb = p_ref[...].astype(BF16)
    part = None
    for cs in _col_chunks(hn_ref.shape[1]):
        z = jnp.dot(a_ref[...], w_ref[:, cs], preferred_element_type=F32) * rstd
        gate = 1.0 / (1.0 + jnp.exp(-z))
        up = jnp.dot(pb, wup_ref[:, cs], preferred_element_type=F32)
        p = _emit_cols(h_ref[:, cs] + gate * up, cs, g_ref, hn_ref, hb_ref)
        part = p if part is None else part + p
    _accumulate_ss(ss_ref, part, pl.program_id(1) == 0)


def _stream_out(t, d, ng, bm, bn):
    specs = [pl.BlockSpec((bm, bn), lambda m, n: (m, n)),
             pl.BlockSpec((ng, bm, bn), lambda m, n: (0, m, n)),
             pl.BlockSpec((bm, LANES), lambda m, n: (m, 0))]
    shapes = [jax.ShapeDtypeStruct((t, d), F32),
              jax.ShapeDtypeStruct((ng, t, d), BF16),
              jax.ShapeDtypeStruct((t, LANES), F32)]
    return specs, shapes


def _resid_matmul(a, w, layer, h, gains, name):
    t, k = a.shape
    d = w.shape[2]
    ng = gains.shape[0]
    bm, bn = 1024, 512
    out_specs, out_shape = _stream_out(t, d, ng, bm, bn)
    return pl.pallas_call(
        _resid_kernel,
        grid=(t // bm, d // bn),
        in_specs=[pl.BlockSpec((bm, k), lambda m, n: (m, 0)),
                  _weight_spec(w, layer, bn),
                  pl.BlockSpec((bm, bn), lambda m, n: (m, n)),
                  pl.BlockSpec((ng, bn), lambda m, n: (0, n))],
        out_specs=out_specs,
        out_shape=out_shape,
        compiler_params=_cparams("parallel", "arbitrary"),
        name=name,
    )(a, w, h, gains)


def _ple(hb, ss, w_gate, p, w_up, layer, h, gains):
    _, t, _ = hb.shape
    d = w_gate.shape[2]
    ng = gains.shape[0]
    pdim = p.shape[-1]
    bm, bn = 1024, 512
    out_specs, out_shape = _stream_out(t, d, ng, bm, bn)
    return pl.pallas_call(
        _ple_kernel,
        grid=(t // bm, d // bn),
        in_specs=_stream_specs(hb, 0, bm) + [
            _weight_spec(w_gate, layer, bn),
            pl.BlockSpec((None, None, bm, pdim), lambda m, n: (layer, 0, m, 0)),
            _weight_spec(w_up, layer, bn),
            pl.BlockSpec((bm, bn), lambda m, n: (m, n)),
            pl.BlockSpec((ng, bn), lambda m, n: (0, n))],
        out_specs=out_specs,
        out_shape=out_shape,
        compiler_params=_cparams("parallel", "arbitrary"),
        name="ple",
    )(hb, ss, w_gate, p, w_up, h, gains)


def _mlp_kernel(a_ref, ssin_ref, w1_ref, w2_ref, h_ref, g_ref, hn_ref, hb_ref, ss_ref, acc_ref,
                *, n_ff):
    f = pl.program_id(1)
    ec = acc_ref.shape[2]

    @pl.when(f == 0)
    def _():
        acc_ref[...] = jnp.zeros_like(acc_ref)

    @pl.when(f < n_ff)
    def _():
        z = jnp.dot(a_ref[...], w1_ref[...], preferred_element_type=F32)
        z = jnp.maximum(z * _rstd(ssin_ref, a_ref.shape[1]), 0.0)
        act = (z * z).astype(BF16)
        for j in range(acc_ref.shape[0]):
            acc_ref[j] += jnp.dot(act, w2_ref[:, j * ec:(j + 1) * ec], preferred_element_type=F32)

    @pl.when(f >= n_ff)
    def _():
        part = _emit_cols(h_ref[...] + acc_ref[f - n_ff], slice(0, ec), g_ref, hn_ref, hb_ref)
        _accumulate_ss(ss_ref, part, f == n_ff)


def _mlp(hb, ss, w1, w2, layer, h, gains):
    _, t, k = hb.shape
    d_ff = w1.shape[2]
    d = w2.shape[2]
    ng = gains.shape[0]
    bm, bf, ec = 512, 512, 1024
    n_ff, n_ec = d_ff // bf, d // ec
    last = n_ff - 1

    def epi(m, f):
        return (m, jnp.maximum(f - n_ff, 0))

    return pl.pallas_call(
        functools.partial(_mlp_kernel, n_ff=n_ff),
        grid=(t // bm, n_ff + n_ec),
        in_specs=_stream_specs(hb, 0, bm) + [
            pl.BlockSpec((None, k, bf), lambda m, f: (layer, 0, jnp.minimum(f, last))),
            pl.BlockSpec((None, bf, d), lambda m, f: (layer, jnp.minimum(f, last), 0)),
            pl.BlockSpec((bm, ec), epi),
            pl.BlockSpec((ng, ec), lambda m, f: (0, jnp.maximum(f - n_ff, 0)))],
        out_specs=[pl.BlockSpec((bm, ec), epi),
                   pl.BlockSpec((ng, bm, ec), lambda m, f: (0, m, jnp.maximum(f - n_ff, 0))),
                   pl.BlockSpec((bm, LANES), lambda m, f: (m, 0))],
        out_shape=[jax.ShapeDtypeStruct((t, d), F32),
                   jax.ShapeDtypeStruct((ng, t, d), BF16),
                   jax.ShapeDtypeStruct((t, LANES), F32)],
        scratch_shapes=[pltpu.VMEM((n_ec, bm, ec), F32)],
        compiler_params=_cparams("parallel", "arbitrary"),
        name="mlp",
    )(hb, ss, w1, w2, h, gains)


def _hgrn_kernel(q_ref, f_ref, i_ref, g_ref, lbl_ref, ong_ref, o_ref, st_ref, b_ref, k_ref,
                 *, layer, unroll):
    bt = q_ref.shape[0]
    c_len, sb = HG_CHUNK, HG_SUB
    n_sb = c_len // sb

    @pl.when(pl.program_id(1) == 0)
    def _():
        st_ref[...] = jnp.zeros_like(st_ref)

    lg = lbl_ref[...]
    e = jnp.exp(lg - jnp.max(lg, axis=0, keepdims=True))
    sm = e / jnp.sum(e, axis=0, keepdims=True)
    cs = sm[0:1]
    for l in range(1, layer + 1):
        cs = cs + sm[l:l + 1]
    lb = cs - sm[0:1]

    sig = 1.0 / (1.0 + jnp.exp(-f_ref[...]))
    fgate = lb + (1.0 - lb) * sig
    k_ref[...] = 1.0 - fgate
    row = lax.broadcasted_iota(jnp.int32, (bt, HG_HEAD_DIM), 0) & (c_len - 1)
    b = jnp.log(fgate)
    shift = 1
    while shift < c_len:
        b = b + jnp.where(row >= shift, pltpu.roll(b, shift, axis=0), 0.0)
        shift *= 2
    b_ref[...] = b

    ends = b_ref[pl.ds(sb - 1, bt // sb, stride=sb), :]
    er = lax.broadcasted_iota(jnp.int32, ends.shape, 0) & (n_sb - 1)
    prev = jnp.where(er == 0, 0.0, pltpu.roll(ends, 1, axis=0))
    safe = jnp.max(prev - ends) <= HG_SAFE_DECAY

    tri = (lax.broadcasted_iota(jnp.int32, (c_len, c_len), 1)
           <= lax.broadcasted_iota(jnp.int32, (c_len, c_len), 0))
    rowc = lax.broadcasted_iota(jnp.int32, (c_len, 1), 0)
    ong = ong_ref[...]

    def chunk(c, fast):
        r0 = pl.multiple_of(c * c_len, c_len)
        sl = pl.ds(r0, c_len)
        q, v, gg, bc, kk = q_ref[sl, :], i_ref[sl, :], g_ref[sl, :], b_ref[sl, :], k_ref[sl, :]
        st = st_ref[...]
        bl = bc[c_len - 1:c_len, :]
        vb = v.astype(BF16)

        o = lax.dot_general((q * jnp.exp(bc)).astype(BF16), st.astype(BF16), DN_LAST,
                            preferred_element_type=F32)

        if fast:
            refs = [jnp.zeros((1, HG_HEAD_DIM), F32)] + [bc[sb * i - 1:sb * i, :] for i in range(1, n_sb)]
            w = jnp.concatenate([bc[sb * i:sb * (i + 1), :] - refs[i] for i in range(n_sb)], axis=0)
            qe = (q * jnp.exp(w)).astype(BF16)
            kb = kk * jnp.exp(-w)
            att_rows = []
            for i in range(n_sb):
                parts = []
                for j in range(n_sb):
                    kj = kb[sb * j:sb * (j + 1), :]
                    if j < i:
                        parts.append(kj * jnp.exp(refs[i] - refs[j]))
                    elif j == i:
                        parts.append(kj)
                    else:
                        parts.append(jnp.zeros_like(kj))
                ki = jnp.concatenate(parts, axis=0).astype(BF16)
                att_rows.append(lax.dot_general(qe[sb * i:sb * (i + 1), :], ki, DN_LAST,
                                                preferred_element_type=F32))
            att = jnp.where(tri, jnp.concatenate(att_rows, axis=0), 0.0)
            o = o + jnp.dot(att.astype(BF16), vb, preferred_element_type=F32)
        else:
            def key_row(s, acc):
                bs = b_ref[pl.ds(r0 + s, 1), :]
                ks = k_ref[pl.ds(r0 + s, 1), :]
                vs = i_ref[pl.ds(r0 + s, 1), :]
                a = jnp.sum(q * ks * jnp.exp(jnp.minimum(bc - bs, 0.0)), axis=-1, keepdims=True)
                return acc + jnp.where(rowc >= s, a, 0.0) * vs

            o = o + lax.fori_loop(0, c_len, key_row, jnp.zeros((c_len, HG_HEAD_DIM), F32))

        kdec = (kk * jnp.exp(bl - bc)).astype(BF16)
        st_ref[...] = st * jnp.exp(bl) + jnp.dot(v.T.astype(BF16), kdec, preferred_element_type=F32)

        on = o * lax.rsqrt(jnp.mean(o * o, axis=-1, keepdims=True) + NORM_EPS)
        silu = gg / (1.0 + jnp.exp(-gg))
        o_ref[sl, :] = (on * ong * silu).astype(BF16)

    n_chunks = bt // c_len

    @pl.when(safe)
    def _():
        def body(c, carry):
            chunk(c, True)
            return carry
        lax.fori_loop(0, n_chunks, body, 0, unroll=unroll)

    @pl.when(jnp.logical_not(safe))
    def _():
        def body(c, carry):
            chunk(c, False)
            return carry
        lax.fori_loop(0, n_chunks, body, 0)


def _hgrn_scan(proj, lb_logits, out_norm_g, layer):
    t = proj.shape[0]
    d = proj.shape[1] // 4
    heads = d // HG_HEAD_DIM
    n_a = lb_logits.shape[0]
    bt = 1024

    def col(part):
        return pl.BlockSpec((bt, HG_HEAD_DIM), lambda h, i: (i, part * heads + h))

    return pl.pallas_call(
        functools.partial(_hgrn_kernel, layer=layer, unroll=8),
        grid=(heads, t // bt),
        in_specs=[col(0), col(1), col(2), col(3),
                  pl.BlockSpec((n_a, HG_HEAD_DIM), lambda h, i: (0, h)),
                  pl.BlockSpec((None, 1, HG_HEAD_DIM), lambda h, i: (layer, 0, h))],
        out_specs=pl.BlockSpec((bt, HG_HEAD_DIM), lambda h, i: (i, h)),
        out_shape=jax.ShapeDtypeStruct((t, d), BF16),
        scratch_shapes=[pltpu.VMEM((HG_HEAD_DIM, HG_HEAD_DIM), F32),
                        pltpu.VMEM((bt, HG_HEAD_DIM), F32),
                        pltpu.VMEM((bt, HG_HEAD_DIM), F32)],
        compiler_params=_cparams("parallel", "arbitrary"),
        name="hgrn_scan",
    )(proj, proj, proj, proj, lb_logits, out_norm_g.reshape(n_a, 1, d))


def _attn_kernel(sink_ref, bias_ref, q_ref, kc_ref, kp_ref, vc_ref, vp_ref, o_ref):
    jj = pl.program_id(1)
    w2 = 2 * WINDOW
    heads_per_kv = q_ref.shape[1] // (2 * ATT_HEAD_DIM)
    n_slab = heads_per_kv // 2
    rows = n_slab * WINDOW

    k2 = jnp.concatenate([kp_ref[...], kc_ref[...]], axis=0)
    v2 = jnp.concatenate([vp_ref[...], vc_ref[...]], axis=0)
    k2r, v2r = pltpu.roll(k2, ATT_HEAD_DIM, axis=1), pltpu.roll(v2, ATT_HEAD_DIM, axis=1)
    low = lax.broadcasted_iota(jnp.int32, k2.shape, 1) < ATT_HEAD_DIM
    ones_lo = jnp.where(low, 1.0, 0.0)
    out_low = lax.broadcasted_iota(jnp.int32, (rows, LANES), 1) < ATT_HEAD_DIM
    bias = bias_ref[...][None]

    for kvh in range(2):
        k_lo, k_hi = (k2, k2r) if kvh == 0 else (k2r, k2)
        v_lo, v_hi = (v2, v2r) if kvh == 0 else (v2r, v2)
        k_half = (jnp.where(low, k_lo, 0.0).astype(BF16), jnp.where(low, 0.0, k_hi).astype(BF16))
        v_ext = (jnp.concatenate([jnp.where(low, v_lo, 0.0), ones_lo], axis=1).astype(BF16),
                 jnp.concatenate([jnp.where(low, 0.0, v_hi), 1.0 - ones_lo], axis=1).astype(BF16))
        col0 = kvh * n_slab * LANES
        qs = jnp.concatenate([q_ref[:, col0 + p * LANES:col0 + (p + 1) * LANES] for p in range(n_slab)],
                             axis=0)
        res, sink_e = [], []
        for parity in range(2):
            s = lax.dot_general(qs, k_half[parity], DN_LAST, preferred_element_type=F32)
            s3 = s.reshape(n_slab, WINDOW, w2) + bias
            head0 = (2 * jj + kvh) * heads_per_kv + parity
            sink3 = jnp.concatenate(
                [jnp.full((1, WINDOW, 1), sink_ref[head0 + 2 * p], F32) for p in range(n_slab)], axis=0)
            m3 = jnp.maximum(jnp.max(s3, axis=-1, keepdims=True), sink3)
            e = jnp.exp(s3 - m3).reshape(rows, w2).astype(BF16)
            sink_e.append(jnp.exp(sink3 - m3).reshape(rows, 1))
            res.append(jnp.dot(e, v_ext[parity], preferred_element_type=F32))
        num = res[0][:, :LANES] + res[1][:, :LANES]
        den = res[0][:, LANES:] + res[1][:, LANES:] + jnp.where(out_low, sink_e[0], sink_e[1])
        out = num / den
        for p in range(n_slab):
            o_ref[:, col0 + p * LANES:col0 + (p + 1) * LANES] = out[p * WINDOW:(p + 1) * WINDOW, :].astype(BF16)


def _attention(q, k, v, sinks):
    t, dq = q.shape
    nb = t // WINDOW
    gw = 2 * dq // ATT_KV_HEADS
    a_idx = jnp.arange(WINDOW)[:, None]
    c_idx = jnp.arange(2 * WINDOW)[None, :]
    band = (c_idx >= a_idx + 1) & (c_idx <= a_idx + WINDOW)
    bias = jnp.stack([jnp.where(band & (c_idx >= WINDOW), 0.0, -jnp.inf),
                      jnp.where(band, 0.0, -jnp.inf)]).astype(F32)

    def cur(n, j):
        return (n, j)

    def prev(n, j):
        return (jnp.maximum(n - 1, 0), j)

    return pl.pallas_call(
        _attn_kernel,
        grid=(nb, ATT_KV_HEADS // 2),
        in_specs=[pl.BlockSpec(memory_space=pltpu.SMEM),
                  pl.BlockSpec((None, WINDOW, 2 * WINDOW), lambda n, j: (jnp.minimum(n, 1), 0, 0)),
                  pl.BlockSpec((WINDOW, gw), lambda n, j: (n, j)),
                  pl.BlockSpec((WINDOW, LANES), cur),
                  pl.BlockSpec((WINDOW, LANES), prev),
                  pl.BlockSpec((WINDOW, LANES), cur),
                  pl.BlockSpec((WINDOW, LANES), prev)],
        out_specs=pl.BlockSpec((WINDOW, gw), lambda n, j: (n, j)),
        out_shape=jax.ShapeDtypeStruct((t, dq), BF16),
        compiler_params=_cparams("parallel", "parallel"),
        name="swa_attention",
    )(sinks, bias, q, k, k, v, v)


def kernel(x, p, positions, mixer_norm_g, hgrn_w_in, hgrn_w_out, hgrn_lb_logits, hgrn_out_norm_g,
           kv_norm_g, w_kv, attn_w_q, attn_w_o, attn_sinks, mlp_norm_g, mlp_w1, mlp_w2,
           ple_norm_g, ple_w_gate, ple_w_up, final_norm_g):
    batch, t, d = x.shape
    depth = mlp_w1.shape[0]
    n_a = hgrn_w_in.shape[0]
    assert batch == 1

    w_in, w_out = hgrn_w_in.astype(BF16), hgrn_w_out.astype(BF16)
    w_q, w_o, w_kvb = attn_w_q.astype(BF16), attn_w_o.astype(BF16), w_kv.astype(BF16)[None]
    w1, w2 = mlp_w1.astype(BF16), mlp_w2.astype(BF16)
    w_gate, w_up = ple_w_gate.astype(BF16), ple_w_up.astype(BF16)

    cos, sin = _rope_tables(positions)
    h = x.reshape(t, d)
    hb, ss = _prep_stream(h, mixer_norm_g[0])
    k_sh = v_sh = None

    for layer in range(depth):
        mlp_gain = mlp_norm_g[layer][None, :]
        if layer < n_a:
            proj = _proj(hb, ss, 0, w_in, layer, F32, "hgrn_in_proj")
            mixed = _hgrn_scan(proj, hgrn_lb_logits, hgrn_out_norm_g, layer)
            h, hb, ss = _resid_matmul(mixed, w_out, layer, h, mlp_gain, "hgrn_out_proj")
        else:
            j = layer - n_a
            q = _proj_rope(hb, ss, 0, w_q, j, cos, sin, ATT_HEAD_DIM ** -0.5)
            mixed = _attention(q, k_sh, v_sh, attn_sinks[j])
            h, hb, ss = _resid_matmul(mixed, w_o, j, h, mlp_gain, "attn_out_proj")

        h, hb, ss = _mlp(hb, ss, w1, w2, layer, h, ple_norm_g[layer][None, :])

        if layer + 1 < depth:
            next_gains = [mixer_norm_g[layer + 1]]
            if layer == n_a - 1:
                next_gains.append(kv_norm_g)
        else:
            next_gains = [final_norm_g]
        h, hb, ss = _ple(hb, ss, w_gate, p, w_up, layer, h, jnp.stack(next_gains))

        if layer == n_a - 1:
            k_sh, v_sh = _kv_proj(hb, ss, 1, w_kvb, cos, sin)

    return _final_norm(h, ss, final_norm_g).reshape(batch, t, d)
```

```python
import functools

import jax
import jax.numpy as jnp
from jax import lax
from jax.experimental import pallas as pl
from jax.experimental.pallas import tpu as pltpu

F32 = jnp.float32
BF16 = jnp.bfloat16

LANES = 128
MXU_COLS = 256
VMEM_LIMIT_BYTES = 56 * 1024 * 1024
MLP_VMEM_LIMIT_BYTES = 60 * 1024 * 1024

HG_HEAD_DIM = 128
HG_CHUNK = 64
HG_SAFE_DECAY = 60.0
ATT_HEAD_DIM = 64
ATT_KV_HEADS = 8
WINDOW = 128
ROPE_THETA = 10000.0
NORM_EPS = 1e-6

DN_LAST = (((1,), (1,)), ((), ()))


def _cparams(*sem):
    return pltpu.CompilerParams(dimension_semantics=sem, vmem_limit_bytes=VMEM_LIMIT_BYTES)


def _rstd(ss_ref, d_model):
    return lax.rsqrt(jnp.sum(ss_ref[...], axis=-1, keepdims=True) * (1.0 / d_model) + NORM_EPS)


def _lane_partial_sumsq(x):
    sq = x * x
    acc = sq[:, 0:LANES]
    for c in range(1, x.shape[1] // LANES):
        acc = acc + sq[:, c * LANES:(c + 1) * LANES]
    return acc


def _col_chunks(width):
    return [slice(c, c + MXU_COLS) for c in range(0, width, MXU_COLS)]


def _emit_cols(h_new, cs, g_ref, hn_ref, hb_ref):
    hn_ref[:, cs] = h_new
    for i in range(hb_ref.shape[0]):
        hb_ref[i, :, cs] = (h_new * g_ref[i:i + 1, cs]).astype(BF16)
    return _lane_partial_sumsq(h_new)


def _accumulate_ss(ss_ref, part, first):
    @pl.when(first)
    def _():
        ss_ref[...] = jnp.zeros_like(ss_ref)
    ss_ref[...] += part


def _swap_half_heads(x):
    lane = lax.broadcasted_iota(jnp.int32, x.shape, 1)
    first_half = (lane & (ATT_HEAD_DIM - 1)) < (ATT_HEAD_DIM // 2)
    return jnp.where(first_half, pltpu.roll(x, LANES - ATT_HEAD_DIM // 2, axis=1),
                     pltpu.roll(x, ATT_HEAD_DIM // 2, axis=1))


def _rope_store(x, cos, sin_signed, o_ref, col0):
    for c in range(x.shape[1] // LANES):
        xc = x[:, c * LANES:(c + 1) * LANES]
        piece = xc * cos + _swap_half_heads(xc) * sin_signed
        o_ref[:, col0 + c * LANES:col0 + (c + 1) * LANES] = piece.astype(o_ref.dtype)


def _rope_table_kernel(pos_ref, invf_ref, sign_ref, cos_ref, sin_ref):
    ang = pos_ref[...].astype(F32) * invf_ref[...]
    cos_ref[...] = jnp.cos(ang)
    sin_ref[...] = jnp.sin(ang) * sign_ref[...]


def _rope_tables(positions):
    t = positions.shape[-1]
    half = ATT_HEAD_DIM // 2
    inv_freq = ROPE_THETA ** (-jnp.arange(half, dtype=F32) / half)
    invf = jnp.tile(inv_freq, LANES // half)[None, :]
    sign = jnp.tile(jnp.concatenate([-jnp.ones((half,), F32), jnp.ones((half,), F32)]),
                    LANES // ATT_HEAD_DIM)[None, :]
    bt = 1024
    return pl.pallas_call(
        _rope_table_kernel,
        grid=(t // bt,),
        in_specs=[pl.BlockSpec((bt, 1), lambda i: (i, 0)),
                  pl.BlockSpec((1, LANES), lambda i: (0, 0)),
                  pl.BlockSpec((1, LANES), lambda i: (0, 0))],
        out_specs=[pl.BlockSpec((bt, LANES), lambda i: (i, 0))] * 2,
        out_shape=[jax.ShapeDtypeStruct((t, LANES), F32)] * 2,
        compiler_params=_cparams("parallel"),
        name="rope_tables",
    )(positions.reshape(t, 1), invf, sign)


def _prep_kernel(x_ref, g_ref, hb_ref, ss_ref):
    x = x_ref[...]
    hb_ref[0] = (x * g_ref[...]).astype(BF16)
    ss_ref[...] = _lane_partial_sumsq(x)


def _prep_stream(x, g):
    t, d = x.shape
    bt = 256
    return pl.pallas_call(
        _prep_kernel,
        grid=(t // bt,),
        in_specs=[pl.BlockSpec((bt, d), lambda i: (i, 0)),
                  pl.BlockSpec((1, d), lambda i: (0, 0))],
        out_specs=[pl.BlockSpec((1, bt, d), lambda i: (0, i, 0)),
                   pl.BlockSpec((bt, LANES), lambda i: (i, 0))],
        out_shape=[jax.ShapeDtypeStruct((1, t, d), BF16),
                   jax.ShapeDtypeStruct((t, LANES), F32)],
        compiler_params=_cparams("parallel"),
        name="prep_stream",
    )(x, g.reshape(1, d))


def _final_norm_kernel(h_ref, ss_ref, g_ref, o_ref):
    o_ref[...] = h_ref[...] * _rstd(ss_ref, h_ref.shape[1]) * g_ref[...]


def _final_norm(h, ss, g):
    t, d = h.shape
    bt = 256
    return pl.pallas_call(
        _final_norm_kernel,
        grid=(t // bt,),
        in_specs=[pl.BlockSpec((bt, d), lambda i: (i, 0)),
                  pl.BlockSpec((bt, LANES), lambda i: (i, 0)),
                  pl.BlockSpec((1, d), lambda i: (0, 0))],
        out_specs=pl.BlockSpec((bt, d), lambda i: (i, 0)),
        out_shape=jax.ShapeDtypeStruct((t, d), F32),
        compiler_params=_cparams("parallel"),
        name="final_norm",
    )(h, ss, g.reshape(1, d))


def _proj_kernel(a_ref, ss_ref, w_ref, o_ref):
    acc = jnp.dot(a_ref[...], w_ref[...], preferred_element_type=F32)
    o_ref[...] = (acc * _rstd(ss_ref, a_ref.shape[1])).astype(o_ref.dtype)


def _proj_rope_kernel(a_ref, ss_ref, w_ref, cos_ref, sin_ref, o_ref, *, scale):
    row_scale = _rstd(ss_ref, a_ref.shape[1]) * scale
    cos, sin = cos_ref[...], sin_ref[...]
    for cs in _col_chunks(o_ref.shape[1]):
        acc = jnp.dot(a_ref[...], w_ref[:, cs], preferred_element_type=F32)
        _rope_store(acc * row_scale, cos, sin, o_ref, cs.start)


def _kv_kernel(a_ref, ss_ref, w_ref, cos_ref, sin_ref, k_ref, v_ref):
    rstd = _rstd(ss_ref, a_ref.shape[1])
    nk = k_ref.shape[1]
    cos, sin = cos_ref[...], sin_ref[...]
    for cs in _col_chunks(nk):
        acc = jnp.dot(a_ref[...], w_ref[:, cs], preferred_element_type=F32)
        _rope_store(acc * rstd, cos, sin, k_ref, cs.start)
    for cs in _col_chunks(nk):
        acc = jnp.dot(a_ref[...], w_ref[:, nk + cs.start:nk + cs.stop], preferred_element_type=F32)
        v_ref[:, cs] = acc * rstd


def _stream_specs(hb, idx, bm):
    k = hb.shape[2]
    return [pl.BlockSpec((None, bm, k), lambda m, n: (idx, m, 0)),
            pl.BlockSpec((bm, LANES), lambda m, n: (m, 0))]


def _weight_spec(w, layer, bn):
    return pl.BlockSpec((None, w.shape[1], bn), lambda m, n: (layer, 0, n))


def _proj(hb, ss, idx, w, layer, out_dtype, name):
    _, t, _ = hb.shape
    n = w.shape[2]
    bm, bn = 1024, 1024
    return pl.pallas_call(
        _proj_kernel,
        grid=(t // bm, n // bn),
        in_specs=_stream_specs(hb, idx, bm) + [_weight_spec(w, layer, bn)],
        out_specs=pl.BlockSpec((bm, bn), lambda m, n: (m, n)),
        out_shape=jax.ShapeDtypeStruct((t, n), out_dtype),
        compiler_params=_cparams("parallel", "parallel"),
        name=name,
    )(hb, ss, w)


def _proj_rope(hb, ss, idx, w, layer, cos, sin, scale):
    _, t, _ = hb.shape
    n = w.shape[2]
    bm, bn = 1024, 1024
    return pl.pallas_call(
        functools.partial(_proj_rope_kernel, scale=scale),
        grid=(t // bm, n // bn),
        in_specs=_stream_specs(hb, idx, bm) + [
            _weight_spec(w, layer, bn),
            pl.BlockSpec((bm, LANES), lambda m, n: (m, 0)),
            pl.BlockSpec((bm, LANES), lambda m, n: (m, 0))],
        out_specs=pl.BlockSpec((bm, bn), lambda m, n: (m, n)),
        out_shape=jax.ShapeDtypeStruct((t, n), BF16),
        compiler_params=_cparams("parallel", "parallel"),
        name="q_proj_rope",
    )(hb, ss, w, cos, sin)


def _kv_proj(hb, ss, idx, w, cos, sin):
    _, t, _ = hb.shape
    n = w.shape[2]
    nk = n // 2
    bm = 1024
    return pl.pallas_call(
        _kv_kernel,
        grid=(t // bm, 1),
        in_specs=_stream_specs(hb, idx, bm) + [
            _weight_spec(w, 0, n),
            pl.BlockSpec((bm, LANES), lambda m, n: (m, 0)),
            pl.BlockSpec((bm, LANES), lambda m, n: (m, 0))],
        out_specs=[pl.BlockSpec((bm, nk), lambda m, n: (m, 0))] * 2,
        out_shape=[jax.ShapeDtypeStruct((t, nk), F32)] * 2,
        compiler_params=_cparams("parallel", "arbitrary"),
        name="kv_proj_rope",
    )(hb, ss, w, cos, sin)


def _resid_kernel(a_ref, w_ref, h_ref, g_ref, hn_ref, hb_ref, ss_ref):
    part = None
    for cs in _col_chunks(hn_ref.shape[1]):
        acc = jnp.dot(a_ref[...], w_ref[:, cs], preferred_element_type=F32)
        p = _emit_cols(h_ref[:, cs] + acc, cs, g_ref, hn_ref, hb_ref)
        part = p if part is None else part + p
    _accumulate_ss(ss_ref, part, pl.program_id(1) == 0)


def _ple_kernel(a_ref, ssin_ref, w_ref, p_ref, wup_ref, h_ref, g_ref, hn_ref, hb_ref, ss_ref):
    rstd = _rstd(ssin_ref, a_ref.shape[1])
    pb = p_ref[...].astype(BF16)
    part = None
    for cs in _col_chunks(hn_ref.shape[1]):
        z = jnp.dot(a_ref[...], w_ref[:, cs], preferred_element_type=F32) * rstd
        gate = 1.0 / (1.0 + jnp.exp(-z))
        up = jnp.dot(pb, wup_ref[:, cs], preferred_element_type=F32)
        p = _emit_cols(h_ref[:, cs] + gate * up, cs, g_ref, hn_ref, hb_ref)
        part = p if part is None else part + p
    _accumulate_ss(ss_ref, part, pl.program_id(1) == 0)


def _stream_out(t, d, ng, bm, bn):
    specs = [pl.BlockSpec((bm, bn), lambda m, n: (m, n)),
             pl.BlockSpec((ng, bm, bn), lambda m, n: (0, m, n)),
             pl.BlockSpec((bm, LANES), lambda m, n: (m, 0))]
    shapes = [jax.ShapeDtypeStruct((t, d), F32),
              jax.ShapeDtypeStruct((ng, t, d), BF16),
              jax.ShapeDtypeStruct((t, LANES), F32)]
    return specs, shapes


def _resid_matmul(a, w, layer, h, gains, name):
    t, k = a.shape
    d = w.shape[2]
    ng = gains.shape[0]
    bm, bn = 1024, 512
    out_specs, out_shape = _stream_out(t, d, ng, bm, bn)
    return pl.pallas_call(
        _resid_kernel,
        grid=(t // bm, d // bn),
        in_specs=[pl.BlockSpec((bm, k), lambda m, n: (m, 0)),
                  _weight_spec(w, layer, bn),
                  pl.BlockSpec((bm, bn), lambda m, n: (m, n)),
                  pl.BlockSpec((ng, bn), lambda m, n: (0, n))],
        out_specs=out_specs,
        out_shape=out_shape,
        compiler_params=_cparams("parallel", "arbitrary"),
        name=name,
    )(a, w, h, gains)


def _ple(hb, ss, w_gate, p, w_up, layer, h, gains):
    _, t, _ = hb.shape
    d = w_gate.shape[2]
    ng = gains.shape[0]
    pdim = p.shape[-1]
    bm, bn = 1024, 512
    out_specs, out_shape = _stream_out(t, d, ng, bm, bn)
    return pl.pallas_call(
        _ple_kernel,
        grid=(t // bm, d // bn),
        in_specs=_stream_specs(hb, 0, bm) + [
            _weight_spec(w_gate, layer, bn),
            pl.BlockSpec((None, None, bm, pdim), lambda m, n: (layer, 0, m, 0)),
            _weight_spec(w_up, layer, bn),
            pl.BlockSpec((bm, bn), lambda m, n: (m, n)),
            pl.BlockSpec((ng, bn), lambda m, n: (0, n))],
        out_specs=out_specs,
        out_shape=out_shape,
        compiler_params=_cparams("parallel", "arbitrary"),
        name="ple",
    )(hb, ss, w_gate, p, w_up, h, gains)


def _mlp_kernel(a_ref, ssin_ref, w1_ref, w2_ref, h_ref, g_ref, hn_ref, hb_ref, ss_ref, acc_ref,
                *, n_ff):
    f = pl.program_id(1)
    ec = acc_ref.shape[2]

    @pl.when(f == 0)
    def _():
        acc_ref[...] = jnp.zeros_like(acc_ref)

    @pl.when(f < n_ff)
    def _():
        z = jnp.dot(a_ref[...], w1_ref[...], preferred_element_type=F32)
        z = jnp.maximum(z * _rstd(ssin_ref, a_ref.shape[1]), 0.0)
        act = (z * z).astype(BF16)
        for j in range(acc_ref.shape[0]):
            acc_ref[j] += jnp.dot(act, w2_ref[:, j * ec:(j + 1) * ec], preferred_element_type=F32)

    @pl.when(f >= n_ff)
    def _():
        part = _emit_cols(h_ref[...] + acc_ref[f - n_ff], slice(0, ec), g_ref, hn_ref, hb_ref)
        _accumulate_ss(ss_ref, part, f == n_ff)


def _mlp(hb, ss, w1, w2, layer, h, gains):
    _, t, k = hb.shape
    d_ff = w1.shape[2]
    d = w2.shape[2]
    ng = gains.shape[0]
    bm, bf, ec = 1024, 512, 512
    n_ff, n_ec = d_ff // bf, d // ec
    last = n_ff - 1

    def epi(m, f):
        return (m, jnp.maximum(f - n_ff, 0))

    return pl.pallas_call(
        functools.partial(_mlp_kernel, n_ff=n_ff),
        grid=(t // bm, n_ff + n_ec),
        in_specs=[
            pl.BlockSpec((None, bm, k), lambda m, f: (0, m, 0), pipeline_mode=pl.Buffered(1)),
            pl.BlockSpec((bm, LANES), lambda m, f: (m, 0)),
            pl.BlockSpec((None, k, bf), lambda m, f: (layer, 0, jnp.minimum(f, last))),
            pl.BlockSpec((None, bf, d), lambda m, f: (layer, jnp.minimum(f, last), 0)),
            pl.BlockSpec((bm, ec), epi),
            pl.BlockSpec((ng, ec), lambda m, f: (0, jnp.maximum(f - n_ff, 0)))],
        out_specs=[pl.BlockSpec((bm, ec), epi),
                   pl.BlockSpec((ng, bm, ec), lambda m, f: (0, m, jnp.maximum(f - n_ff, 0))),
                   pl.BlockSpec((bm, LANES), lambda m, f: (m, 0))],
        out_shape=[jax.ShapeDtypeStruct((t, d), F32),
                   jax.ShapeDtypeStruct((ng, t, d), BF16),
                   jax.ShapeDtypeStruct((t, LANES), F32)],
        scratch_shapes=[pltpu.VMEM((n_ec, bm, ec), F32)],
        compiler_params=pltpu.CompilerParams(dimension_semantics=("parallel", "arbitrary"),
                                             vmem_limit_bytes=MLP_VMEM_LIMIT_BYTES),
        name="mlp",
    )(hb, ss, w1, w2, h, gains)


def _hgrn_kernel(q_ref, f_ref, i_ref, g_ref, lbl_ref, ong_ref, o_ref,
                 st_ref, b_ref, k_ref, qin_ref, qe_ref, ke_ref, kd_ref, dl_ref, gate_ref, oacc_ref, upd_ref,
                 *, layer):
    bt = q_ref.shape[0]
    c_len = HG_CHUNK
    half = c_len // 2
    n_chunks = bt // c_len

    @pl.when(pl.program_id(1) == 0)
    def _():
        st_ref[...] = jnp.zeros_like(st_ref)

    lg = lbl_ref[...]
    e = jnp.exp(lg - jnp.max(lg, axis=0, keepdims=True))
    sm = e / jnp.sum(e, axis=0, keepdims=True)
    cs = sm[0:1]
    for l in range(1, layer + 1):
        cs = cs + sm[l:l + 1]
    lb = cs - sm[0:1]

    sig = 1.0 / (1.0 + jnp.exp(-f_ref[...]))
    fgate = lb + (1.0 - lb) * sig
    kk = 1.0 - fgate
    k_ref[...] = kk
    logf = jnp.log(fgate)
    row = lax.broadcasted_iota(jnp.int32, (c_len, HG_HEAD_DIM), 0)
    shifts = [1 << s for s in range(c_len.bit_length() - 1)]
    keep = [row >= shift for shift in shifts]

    for c in range(n_chunks):
        rows = slice(c * c_len, (c + 1) * c_len)
        bc, kc, q = logf[rows], kk[rows], q_ref[rows, :]
        for shift, mask in zip(shifts, keep):
            bc = bc + jnp.where(mask, pltpu.roll(bc, shift, axis=0), 0.0)
        b_ref[rows, :] = bc
        bl = bc[c_len - 1:c_len, :]
        w = bc - bc[half - 1:half, :]
        qin_ref[rows, :] = (q * jnp.exp(bc)).astype(BF16)
        qe_ref[rows, :] = (q * jnp.exp(w)).astype(BF16)
        ke_ref[rows, :] = (kc * jnp.exp(-w)).astype(BF16)
        kd_ref[rows, :] = (kc * jnp.exp(bl - bc)).astype(BF16)
        dl_ref[c:c + 1, :] = jnp.exp(bl)
    gg = g_ref[...]
    gate_ref[...] = ong_ref[...] * gg / (1.0 + jnp.exp(-gg))

    mids = b_ref[pl.ds(half - 1, bt // half, stride=half), :]
    second = (lax.broadcasted_iota(jnp.int32, mids.shape, 0) & 1) == 1
    safe = jnp.max(jnp.where(second, pltpu.roll(mids, 1, axis=0), 0.0) - mids) <= HG_SAFE_DECAY

    tri = (lax.broadcasted_iota(jnp.int32, (c_len, c_len), 1)
           <= lax.broadcasted_iota(jnp.int32, (c_len, c_len), 0))
    rowc = lax.broadcasted_iota(jnp.int32, (c_len, 1), 0)

    def state_increment(sl):
        return jnp.dot(i_ref[sl, :].T.astype(BF16), kd_ref[sl, :], preferred_element_type=F32)

    def finish():
        o = oacc_ref[...]
        on = o * lax.rsqrt(jnp.mean(o * o, axis=-1, keepdims=True) + NORM_EPS)
        o_ref[...] = (on * gate_ref[...]).astype(BF16)

    @pl.when(safe)
    def _():
        sls = [slice(c * c_len, (c + 1) * c_len) for c in range(n_chunks)]
        atts = [lax.dot_general(qe_ref[sl, :], ke_ref[sl, :], DN_LAST, preferred_element_type=F32)
                for sl in sls]
        atts = [jnp.where(tri, att, 0.0).astype(BF16) for att in atts]
        for sl, att in zip(sls, atts):
            oacc_ref[sl, :] = jnp.dot(att, i_ref[sl, :].astype(BF16), preferred_element_type=F32)
        for c, sl in enumerate(sls):
            upd_ref[c] = state_increment(sl)
        st = st_ref[...]
        for c, sl in enumerate(sls):
            oacc_ref[sl, :] += lax.dot_general(qin_ref[sl, :], st.astype(BF16), DN_LAST,
                                               preferred_element_type=F32)
            st = st * dl_ref[c:c + 1, :] + upd_ref[c]
        st_ref[...] = st
        finish()

    @pl.when(jnp.logical_not(safe))
    def _():
        def body(c, carry):
            r0 = pl.multiple_of(c * c_len, c_len)
            sl = pl.ds(r0, c_len)
            st = st_ref[...]
            q, bc = q_ref[sl, :], b_ref[sl, :]

            def key_row(s, acc):
                bs = b_ref[pl.ds(r0 + s, 1), :]
                ks = k_ref[pl.ds(r0 + s, 1), :]
                vs = i_ref[pl.ds(r0 + s, 1), :]
                a = jnp.sum(q * ks * jnp.exp(jnp.minimum(bc - bs, 0.0)), axis=-1, keepdims=True)
                return acc + jnp.where(rowc >= s, a, 0.0) * vs

            o = lax.dot_general(qin_ref[sl, :], st.astype(BF16), DN_LAST, preferred_element_type=F32)
            oacc_ref[sl, :] = o + lax.fori_loop(0, c_len, key_row, jnp.zeros((c_len, HG_HEAD_DIM), F32))
            st_ref[...] = st * dl_ref[pl.ds(c, 1), :] + state_increment(sl)
            return carry
        lax.fori_loop(0, n_chunks, body, 0)
        finish()


def _hgrn_scan(proj, lb_logits, out_norm_g, layer):
    t = proj.shape[0]
    d = proj.shape[1] // 4
    heads = d // HG_HEAD_DIM
    n_a = lb_logits.shape[0]
    bt = 1024

    def col(part):
        return pl.BlockSpec((bt, HG_HEAD_DIM), lambda h, i: (i, part * heads + h))

    def rows(dtype):
        return pltpu.VMEM((bt, HG_HEAD_DIM), dtype)

    return pl.pallas_call(
        functools.partial(_hgrn_kernel, layer=layer),
        grid=(heads, t // bt),
        in_specs=[col(0), col(1), col(2), col(3),
                  pl.BlockSpec((n_a, HG_HEAD_DIM), lambda h, i: (0, h)),
                  pl.BlockSpec((None, 1, HG_HEAD_DIM), lambda h, i: (layer, 0, h))],
        out_specs=pl.BlockSpec((bt, HG_HEAD_DIM), lambda h, i: (i, h)),
        out_shape=jax.ShapeDtypeStruct((t, d), BF16),
        scratch_shapes=[pltpu.VMEM((HG_HEAD_DIM, HG_HEAD_DIM), F32),
                        rows(F32), rows(F32),
                        rows(BF16), rows(BF16), rows(BF16), rows(BF16),
                        pltpu.VMEM((bt // HG_CHUNK, HG_HEAD_DIM), F32),
                        rows(F32), rows(F32),
                        pltpu.VMEM((bt // HG_CHUNK, HG_HEAD_DIM, HG_HEAD_DIM), F32)],
        compiler_params=_cparams("parallel", "arbitrary"),
        name="hgrn_scan",
    )(proj, proj, proj, proj, lb_logits, out_norm_g.reshape(n_a, 1, d))


def _attn_kernel(sink_ref, bias_ref, q_ref, kc_ref, kp_ref, vc_ref, vp_ref, o_ref):
    jj = pl.program_id(1)
    heads_per_kv = q_ref.shape[1] // (2 * ATT_HEAD_DIM)
    n_slab = heads_per_kv // 2

    k2 = jnp.concatenate([kp_ref[...], kc_ref[...]], axis=0)
    v2 = jnp.concatenate([vp_ref[...], vc_ref[...]], axis=0)
    k2r, v2r = pltpu.roll(k2, ATT_HEAD_DIM, axis=1), pltpu.roll(v2, ATT_HEAD_DIM, axis=1)
    low = lax.broadcasted_iota(jnp.int32, k2.shape, 1) < ATT_HEAD_DIM
    ones_lo = jnp.where(low, 1.0, 0.0)
    real_key = lax.broadcasted_iota(jnp.int32, k2.shape, 0) > 0
    v_low, v_high = low & real_key, jnp.logical_not(low) & real_key
    sink_col = lax.broadcasted_iota(jnp.int32, (WINDOW, LANES), 1) == 0
    bias = bias_ref[...]

    scores, v_exts = [], []
    for kvh in range(2):
        k_lo, k_hi = (k2, k2r) if kvh == 0 else (k2r, k2)
        v_lo, v_hi = (v2, v2r) if kvh == 0 else (v2r, v2)
        k_half = (jnp.where(low, k_lo, 0.0).astype(BF16), jnp.where(low, 0.0, k_hi).astype(BF16))
        v_exts.append((jnp.concatenate([jnp.where(v_low, v_lo, 0.0), ones_lo], axis=1).astype(BF16),
                       jnp.concatenate([jnp.where(v_high, v_hi, 0.0), 1.0 - ones_lo], axis=1).astype(BF16)))
        col0 = kvh * n_slab * LANES
        qs = jnp.concatenate([q_ref[:, col0 + p * LANES:col0 + (p + 1) * LANES] for p in range(n_slab)],
                             axis=0)
        scores.append([lax.dot_general(qs, k_half[parity], DN_LAST, preferred_element_type=F32)
                       for parity in range(2)])

    probs = []
    for kvh in range(2):
        pe = []
        for parity in range(2):
            head0 = (2 * jj + kvh) * heads_per_kv + parity
            slabs = []
            for p in range(n_slab):
                sp = scores[kvh][parity][p * WINDOW:(p + 1) * WINDOW, :] + bias
                sp = jnp.concatenate([jnp.where(sink_col, sink_ref[head0 + 2 * p], sp[:, :LANES]),
                                      sp[:, LANES:]], axis=1)
                slabs.append(jnp.exp(sp - jnp.max(sp, axis=-1, keepdims=True)).astype(BF16))
            pe.append(jnp.concatenate(slabs, axis=0))
        probs.append(pe)

    for kvh in range(2):
        res = [jnp.dot(probs[kvh][parity], v_exts[kvh][parity], preferred_element_type=F32)
               for parity in range(2)]
        out = (res[0][:, :LANES] + res[1][:, :LANES]) / (res[0][:, LANES:] + res[1][:, LANES:])
        col0 = kvh * n_slab * LANES
        for p in range(n_slab):
            o_ref[:, col0 + p * LANES:col0 + (p + 1) * LANES] = out[p * WINDOW:(p + 1) * WINDOW, :].astype(BF16)


def _attention(q, k, v, sinks):
    t, dq = q.shape
    nb = t // WINDOW
    gw = 2 * dq // ATT_KV_HEADS
    a_idx = jnp.arange(WINDOW)[:, None]
    c_idx = jnp.arange(2 * WINDOW)[None, :]
    band = (c_idx >= a_idx + 1) & (c_idx <= a_idx + WINDOW)
    bias = jnp.stack([jnp.where(band & (c_idx >= WINDOW), 0.0, -jnp.inf),
                      jnp.where(band, 0.0, -jnp.inf)]).astype(F32)

    def cur(n, j):
        return (n, j)

    def prev(n, j):
        return (jnp.maximum(n - 1, 0), j)

    return pl.pallas_call(
        _attn_kernel,
        grid=(nb, ATT_KV_HEADS // 2),
        in_specs=[pl.BlockSpec(memory_space=pltpu.SMEM),
                  pl.BlockSpec((None, WINDOW, 2 * WINDOW), lambda n, j: (jnp.minimum(n, 1), 0, 0)),
                  pl.BlockSpec((WINDOW, gw), lambda n, j: (n, j)),
                  pl.BlockSpec((WINDOW, LANES), cur),
                  pl.BlockSpec((WINDOW, LANES), prev),
                  pl.BlockSpec((WINDOW, LANES), cur),
                  pl.BlockSpec((WINDOW, LANES), prev)],
        out_specs=pl.BlockSpec((WINDOW, gw), lambda n, j: (n, j)),
        out_shape=jax.ShapeDtypeStruct((t, dq), BF16),
        compiler_params=_cparams("parallel", "parallel"),
        name="swa_attention",
    )(sinks, bias, q, k, k, v, v)


def kernel(x, p, positions, mixer_norm_g, hgrn_w_in, hgrn_w_out, hgrn_lb_logits, hgrn_out_norm_g,
           kv_norm_g, w_kv, attn_w_q, attn_w_o, attn_sinks, mlp_norm_g, mlp_w1, mlp_w2,
           ple_norm_g, ple_w_gate, ple_w_up, final_norm_g):
    batch, t, d = x.shape
    depth = mlp_w1.shape[0]
    n_a = hgrn_w_in.shape[0]
    assert batch == 1

    w_in, w_out = hgrn_w_in.astype(BF16), hgrn_w_out.astype(BF16)
    w_q, w_o, w_kvb = attn_w_q.astype(BF16), attn_w_o.astype(BF16), w_kv.astype(BF16)[None]
    w1, w2 = mlp_w1.astype(BF16), mlp_w2.astype(BF16)
    w_gate, w_up = ple_w_gate.astype(BF16), ple_w_up.astype(BF16)

    cos, sin = _rope_tables(positions)
    h = x.reshape(t, d)
    hb, ss = _prep_stream(h, mixer_norm_g[0])
    k_sh = v_sh = None

    for layer in range(depth):
        mlp_gain = mlp_norm_g[layer][None, :]
        if layer < n_a:
            proj = _proj(hb, ss, 0, w_in, layer, F32, "hgrn_in_proj")
            mixed = _hgrn_scan(proj, hgrn_lb_logits, hgrn_out_norm_g, layer)
            h, hb, ss = _resid_matmul(mixed, w_out, layer, h, mlp_gain, "hgrn_out_proj")
        else:
            j = layer - n_a
            q = _proj_rope(hb, ss, 0, w_q, j, cos, sin, ATT_HEAD_DIM ** -0.5)
            mixed = _attention(q, k_sh, v_sh, attn_sinks[j])
            h, hb, ss = _resid_matmul(mixed, w_o, j, h, mlp_gain, "attn_out_proj")

        h, hb, ss = _mlp(hb, ss, w1, w2, layer, h, ple_norm_g[layer][None, :])

        if layer + 1 < depth:
            next_gains = [mixer_norm_g[layer + 1]]
            if layer == n_a - 1:
                next_gains.append(kv_norm_g)
        else:
            next_gains = [final_norm_g]
        h, hb, ss = _ple(hb, ss, w_gate, p, w_up, layer, h, jnp.stack(next_gains))

        if layer == n_a - 1:
            k_sh, v_sh = _kv_proj(hb, ss, 1, w_kvb, cos, sin)

    return _final_norm(h, ss, final_norm_g).reshape(batch, t, d)
```

```python
import functools

import jax
import jax.numpy as jnp
from jax import lax
from jax.experimental import pallas as pl
from jax.experimental.pallas import tpu as pltpu

F32 = jnp.float32
BF16 = jnp.bfloat16

LANES = 128
MXU_COLS = 256
VMEM_LIMIT_BYTES = 56 * 1024 * 1024
BIG_VMEM_LIMIT_BYTES = 60 * 1024 * 1024

HG_HEAD_DIM = 128
HG_CHUNK = 64
HG_SAFE_DECAY = 60.0
ATT_HEAD_DIM = 64
ATT_KV_HEADS = 8
WINDOW = 128
ROPE_THETA = 10000.0
NORM_EPS = 1e-6

DN_LAST = (((1,), (1,)), ((), ()))


def _cparams(*sem):
    return pltpu.CompilerParams(dimension_semantics=sem, vmem_limit_bytes=VMEM_LIMIT_BYTES)


def _rstd(ss_ref, d_model):
    return lax.rsqrt(jnp.sum(ss_ref[...], axis=-1, keepdims=True) * (1.0 / d_model) + NORM_EPS)


def _lane_partial_sumsq(x):
    sq = x * x
    acc = sq[:, 0:LANES]
    for c in range(1, x.shape[1] // LANES):
        acc = acc + sq[:, c * LANES:(c + 1) * LANES]
    return acc


def _col_chunks(width):
    return [slice(c, c + MXU_COLS) for c in range(0, width, MXU_COLS)]


def _emit_cols(h_new, cs, g_ref, hn_ref, hb_ref):
    hn_ref[:, cs] = h_new
    for i in range(hb_ref.shape[0]):
        hb_ref[i, :, cs] = (h_new * g_ref[i:i + 1, cs]).astype(BF16)
    return _lane_partial_sumsq(h_new)


def _accumulate_ss(ss_ref, part, first):
    @pl.when(first)
    def _():
        ss_ref[...] = jnp.zeros_like(ss_ref)
    ss_ref[...] += part


def _cast_specs(jobs, n_steps, step_of):
    in_specs, out_specs, out_shape = [], [], []
    for src, layer in jobs:
        _, k, n = src.shape
        rows = k // n_steps
        assert rows * n_steps == k and rows % 16 == 0
        in_specs.append(pl.BlockSpec((None, rows, n), lambda *g, layer=layer: (layer, step_of(*g), 0)))
        out_specs.append(pl.BlockSpec((rows, n), lambda *g: (step_of(*g), 0)))
        out_shape.append(jax.ShapeDtypeStruct((k, n), BF16))
    return in_specs, out_specs, out_shape


def _cast_blocks(src_refs, dst_refs):
    for src, dst in zip(src_refs, dst_refs):
        dst[...] = src[...].astype(BF16)


def _swap_half_heads(x):
    lane = lax.broadcasted_iota(jnp.int32, x.shape, 1)
    first_half = (lane & (ATT_HEAD_DIM - 1)) < (ATT_HEAD_DIM // 2)
    return jnp.where(first_half, pltpu.roll(x, LANES - ATT_HEAD_DIM // 2, axis=1),
                     pltpu.roll(x, ATT_HEAD_DIM // 2, axis=1))


def _rope_store(x, cos, sin_signed, o_ref, col0):
    for c in range(x.shape[1] // LANES):
        xc = x[:, c * LANES:(c + 1) * LANES]
        piece = xc * cos + _swap_half_heads(xc) * sin_signed
        o_ref[:, col0 + c * LANES:col0 + (c + 1) * LANES] = piece.astype(o_ref.dtype)


def _rope_table_kernel(pos_ref, invf_ref, sign_ref, cos_ref, sin_ref):
    ang = pos_ref[...].astype(F32) * invf_ref[...]
    cos_ref[...] = jnp.cos(ang)
    sin_ref[...] = jnp.sin(ang) * sign_ref[...]


def _rope_tables(positions):
    t = positions.shape[-1]
    half = ATT_HEAD_DIM // 2
    inv_freq = ROPE_THETA ** (-jnp.arange(half, dtype=F32) / half)
    invf = jnp.tile(inv_freq, LANES // half)[None, :]
    sign = jnp.tile(jnp.concatenate([-jnp.ones((half,), F32), jnp.ones((half,), F32)]),
                    LANES // ATT_HEAD_DIM)[None, :]
    bt = 1024
    return pl.pallas_call(
        _rope_table_kernel,
        grid=(t // bt,),
        in_specs=[pl.BlockSpec((bt, 1), lambda i: (i, 0)),
                  pl.BlockSpec((1, LANES), lambda i: (0, 0)),
                  pl.BlockSpec((1, LANES), lambda i: (0, 0))],
        out_specs=[pl.BlockSpec((bt, LANES), lambda i: (i, 0))] * 2,
        out_shape=[jax.ShapeDtypeStruct((t, LANES), F32)] * 2,
        compiler_params=_cparams("parallel"),
        name="rope_tables",
    )(positions.reshape(t, 1), invf, sign)


def _prep_kernel(x_ref, g_ref, hb_ref, ss_ref):
    x = x_ref[...]
    hb_ref[0] = (x * g_ref[...]).astype(BF16)
    ss_ref[...] = _lane_partial_sumsq(x)


def _prep_stream(x, g):
    t, d = x.shape
    bt = 256
    return pl.pallas_call(
        _prep_kernel,
        grid=(t // bt,),
        in_specs=[pl.BlockSpec((bt, d), lambda i: (i, 0)),
                  pl.BlockSpec((1, d), lambda i: (0, 0))],
        out_specs=[pl.BlockSpec((1, bt, d), lambda i: (0, i, 0)),
                   pl.BlockSpec((bt, LANES), lambda i: (i, 0))],
        out_shape=[jax.ShapeDtypeStruct((1, t, d), BF16),
                   jax.ShapeDtypeStruct((t, LANES), F32)],
        compiler_params=_cparams("parallel"),
        name="prep_stream",
    )(x, g.reshape(1, d))


def _final_norm_kernel(h_ref, ss_ref, g_ref, o_ref):
    o_ref[...] = h_ref[...] * _rstd(ss_ref, h_ref.shape[1]) * g_ref[...]


def _final_norm(h, ss, g):
    t, d = h.shape
    bt = 256
    return pl.pallas_call(
        _final_norm_kernel,
        grid=(t // bt,),
        in_specs=[pl.BlockSpec((bt, d), lambda i: (i, 0)),
                  pl.BlockSpec((bt, LANES), lambda i: (i, 0)),
                  pl.BlockSpec((1, d), lambda i: (0, 0))],
        out_specs=pl.BlockSpec((bt, d), lambda i: (i, 0)),
        out_shape=jax.ShapeDtypeStruct((t, d), F32),
        compiler_params=_cparams("parallel"),
        name="final_norm",
    )(h, ss, g.reshape(1, d))


def _proj_kernel(*refs, n_cast):
    a_ref, ss_ref, w_ref = refs[:3]
    o_ref = refs[3 + n_cast]
    acc = jnp.dot(a_ref[...], w_ref[...], preferred_element_type=F32)
    o_ref[...] = (acc * _rstd(ss_ref, a_ref.shape[1])).astype(o_ref.dtype)
    _cast_blocks(refs[3:3 + n_cast], refs[4 + n_cast:])


def _proj_rope_kernel(a_ref, ss_ref, w_ref, cos_ref, sin_ref, o_ref, *, scale):
    row_scale = _rstd(ss_ref, a_ref.shape[1]) * scale
    cos, sin = cos_ref[...], sin_ref[...]
    for cs in _col_chunks(o_ref.shape[1]):
        acc = jnp.dot(a_ref[...], w_ref[:, cs], preferred_element_type=F32)
        _rope_store(acc * row_scale, cos, sin, o_ref, cs.start)


def _kv_kernel(a_ref, ss_ref, w_ref, cos_ref, sin_ref, k_ref, v_ref):
    rstd = _rstd(ss_ref, a_ref.shape[1])
    nk = k_ref.shape[1]
    cos, sin = cos_ref[...], sin_ref[...]
    for cs in _col_chunks(nk):
        acc = jnp.dot(a_ref[...], w_ref[:, cs], preferred_element_type=F32)
        _rope_store(acc * rstd, cos, sin, k_ref, cs.start)
    for cs in _col_chunks(nk):
        acc = jnp.dot(a_ref[...], w_ref[:, nk + cs.start:nk + cs.stop], preferred_element_type=F32)
        v_ref[:, cs] = acc * rstd


def _stream_specs(hb, idx, bm, single_buffer=False):
    k = hb.shape[2]
    mode = dict(pipeline_mode=pl.Buffered(1)) if single_buffer else {}
    return [pl.BlockSpec((None, bm, k), lambda m, n: (idx, m, 0), **mode),
            pl.BlockSpec((bm, LANES), lambda m, n: (m, 0))]


def _weight_spec(w, layer, bn):
    return pl.BlockSpec((None, w.shape[1], bn), lambda m, n: (layer, 0, n))


def _proj(hb, ss, idx, w, layer, out_dtype, name, cast_jobs=()):
    _, t, _ = hb.shape
    n = w.shape[2]
    bm, bn = 1024, 1024
    n_col = n // bn
    cast_in, cast_out, cast_shape = _cast_specs(cast_jobs, (t // bm) * n_col, lambda m, n: m * n_col + n)
    outs = pl.pallas_call(
        functools.partial(_proj_kernel, n_cast=len(cast_jobs)),
        grid=(t // bm, n_col),
        in_specs=_stream_specs(hb, idx, bm, bool(cast_jobs)) + [_weight_spec(w, layer, bn)] + cast_in,
        out_specs=[pl.BlockSpec((bm, bn), lambda m, n: (m, n))] + cast_out,
        out_shape=[jax.ShapeDtypeStruct((t, n), out_dtype)] + cast_shape,
        compiler_params=pltpu.CompilerParams(dimension_semantics=("parallel", "parallel"),
                                             vmem_limit_bytes=BIG_VMEM_LIMIT_BYTES),
        name=name,
    )(hb, ss, w, *[src for src, _ in cast_jobs])
    return outs[0], [w[None] for w in outs[1:]]


def _proj_rope(hb, ss, idx, w, layer, cos, sin, scale):
    _, t, _ = hb.shape
    n = w.shape[2]
    bm, bn = 1024, 1024
    return pl.pallas_call(
        functools.partial(_proj_rope_kernel, scale=scale),
        grid=(t // bm, n // bn),
        in_specs=_stream_specs(hb, idx, bm) + [
            _weight_spec(w, layer, bn),
            pl.BlockSpec((bm, LANES), lambda m, n: (m, 0)),
            pl.BlockSpec((bm, LANES), lambda m, n: (m, 0))],
        out_specs=pl.BlockSpec((bm, bn), lambda m, n: (m, n)),
        out_shape=jax.ShapeDtypeStruct((t, n), BF16),
        compiler_params=_cparams("parallel", "parallel"),
        name="q_proj_rope",
    )(hb, ss, w, cos, sin)


def _kv_proj(hb, ss, idx, w, cos, sin):
    _, t, _ = hb.shape
    n = w.shape[2]
    nk = n // 2
    bm = 1024
    return pl.pallas_call(
        _kv_kernel,
        grid=(t // bm, 1),
        in_specs=_stream_specs(hb, idx, bm) + [
            _weight_spec(w, 0, n),
            pl.BlockSpec((bm, LANES), lambda m, n: (m, 0)),
            pl.BlockSpec((bm, LANES), lambda m, n: (m, 0))],
        out_specs=[pl.BlockSpec((bm, nk), lambda m, n: (m, 0))] * 2,
        out_shape=[jax.ShapeDtypeStruct((t, nk), F32)] * 2,
        compiler_params=_cparams("parallel", "arbitrary"),
        name="kv_proj_rope",
    )(hb, ss, w, cos, sin)


def _resid_kernel(a_ref, w_ref, h_ref, g_ref, hn_ref, hb_ref, ss_ref):
    part = None
    for cs in _col_chunks(hn_ref.shape[1]):
        acc = jnp.dot(a_ref[...], w_ref[:, cs], preferred_element_type=F32)
        p = _emit_cols(h_ref[:, cs] + acc, cs, g_ref, hn_ref, hb_ref)
        part = p if part is None else part + p
    _accumulate_ss(ss_ref, part, pl.program_id(1) == 0)


def _ple_kernel(a_ref, ssin_ref, w_ref, p_ref, wup_ref, h_ref, g_ref, hn_ref, hb_ref, ss_ref):
    rstd = _rstd(ssin_ref, a_ref.shape[1])
    pb = p_ref[...].astype(BF16)
    part = None
    for cs in _col_chunks(hn_ref.shape[1]):
        z = jnp.dot(a_ref[...], w_ref[:, cs], preferred_element_type=F32) * rstd
        gate = 1.0 / (1.0 + jnp.exp(-z))
        up = jnp.dot(pb, wup_ref[:, cs], preferred_element_type=F32)
        p = _emit_cols(h_ref[:, cs] + gate * up, cs, g_ref, hn_ref, hb_ref)
        part = p if part is None else part + p
    _accumulate_ss(ss_ref, part, pl.program_id(1) == 0)


def _stream_out(t, d, ng, bm, bn):
    specs = [pl.BlockSpec((bm, bn), lambda m, n: (m, n)),
             pl.BlockSpec((ng, bm, bn), lambda m, n: (0, m, n)),
             pl.BlockSpec((bm, LANES), lambda m, n: (m, 0))]
    shapes = [jax.ShapeDtypeStruct((t, d), F32),
              jax.ShapeDtypeStruct((ng, t, d), BF16),
              jax.ShapeDtypeStruct((t, LANES), F32)]
    return specs, shapes


def _resid_matmul(a, w, layer, h, gains, name):
    t, k = a.shape
    d = w.shape[2]
    ng = gains.shape[0]
    bm, bn = 1024, 512
    out_specs, out_shape = _stream_out(t, d, ng, bm, bn)
    return pl.pallas_call(
        _resid_kernel,
        grid=(t // bm, d // bn),
        in_specs=[pl.BlockSpec((bm, k), lambda m, n: (m, 0)),
                  _weight_spec(w, layer, bn),
                  pl.BlockSpec((bm, bn), lambda m, n: (m, n)),
                  pl.BlockSpec((ng, bn), lambda m, n: (0, n))],
        out_specs=out_specs,
        out_shape=out_shape,
        compiler_params=_cparams("parallel", "arbitrary"),
        name=name,
    )(a, w, h, gains)


def _ple(hb, ss, w_gate, gate_layer, p, w_up, layer, h, gains):
    _, t, _ = hb.shape
    d = w_gate.shape[2]
    ng = gains.shape[0]
    pdim = p.shape[-1]
    bm, bn = 1024, 512
    out_specs, out_shape = _stream_out(t, d, ng, bm, bn)
    return pl.pallas_call(
        _ple_kernel,
        grid=(t // bm, d // bn),
        in_specs=_stream_specs(hb, 0, bm) + [
            _weight_spec(w_gate, gate_layer, bn),
            pl.BlockSpec((None, None, bm, pdim), lambda m, n: (layer, 0, m, 0)),
            _weight_spec(w_up, layer, bn),
            pl.BlockSpec((bm, bn), lambda m, n: (m, n)),
            pl.BlockSpec((ng, bn), lambda m, n: (0, n))],
        out_specs=out_specs,
        out_shape=out_shape,
        compiler_params=_cparams("parallel", "arbitrary"),
        name="ple",
    )(hb, ss, w_gate, p, w_up, h, gains)


def _mlp_kernel(*refs, n_ff, n_cast):
    a_ref, ssin_ref, w1_ref, w2_ref, h_ref, g_ref = refs[:6]
    cast_src = refs[6:6 + n_cast]
    hn_ref, hb_ref, ss_ref = refs[6 + n_cast:9 + n_cast]
    cast_dst = refs[9 + n_cast:9 + 2 * n_cast]
    acc_ref = refs[-1]
    f = pl.program_id(1)
    ec = acc_ref.shape[2]

    @pl.when(f == 0)
    def _():
        acc_ref[...] = jnp.zeros_like(acc_ref)

    @pl.when(f < n_ff)
    def _():
        z = jnp.dot(a_ref[...], w1_ref[...], preferred_element_type=F32)
        z = jnp.maximum(z * _rstd(ssin_ref, a_ref.shape[1]), 0.0)
        act = (z * z).astype(BF16)
        for j in range(acc_ref.shape[0]):
            acc_ref[j] += jnp.dot(act, w2_ref[:, j * ec:(j + 1) * ec], preferred_element_type=F32)
        _cast_blocks(cast_src, cast_dst)

    @pl.when(f >= n_ff)
    def _():
        part = _emit_cols(h_ref[...] + acc_ref[f - n_ff], slice(0, ec), g_ref, hn_ref, hb_ref)
        _accumulate_ss(ss_ref, part, f == n_ff)


def _mlp(hb, ss, w1, w2, layer, h, gains, cast_jobs=()):
    _, t, k = hb.shape
    d_ff = w1.shape[2]
    d = w2.shape[2]
    ng = gains.shape[0]
    bm, bf, ec = 1024, 512, 256
    n_ff, n_ec = d_ff // bf, d // ec
    last = n_ff - 1

    def epi(m, f):
        return (m, jnp.maximum(f - n_ff, 0))

    cast_in, cast_out, cast_shape = _cast_specs(cast_jobs, (t // bm) * n_ff,
                                                lambda m, f: m * n_ff + jnp.minimum(f, last))
    outs = pl.pallas_call(
        functools.partial(_mlp_kernel, n_ff=n_ff, n_cast=len(cast_jobs)),
        grid=(t // bm, n_ff + n_ec),
        in_specs=[
            pl.BlockSpec((None, bm, k), lambda m, f: (0, m, 0), pipeline_mode=pl.Buffered(1)),
            pl.BlockSpec((bm, LANES), lambda m, f: (m, 0)),
            pl.BlockSpec((None, k, bf), lambda m, f: (layer, 0, jnp.minimum(f, last))),
            pl.BlockSpec((None, bf, d), lambda m, f: (layer, jnp.minimum(f, last), 0)),
            pl.BlockSpec((bm, ec), epi),
            pl.BlockSpec((ng, ec), lambda m, f: (0, jnp.maximum(f - n_ff, 0)))] + cast_in,
        out_specs=[pl.BlockSpec((bm, ec), epi),
                   pl.BlockSpec((ng, bm, ec), lambda m, f: (0, m, jnp.maximum(f - n_ff, 0))),
                   pl.BlockSpec((bm, LANES), lambda m, f: (m, 0))] + cast_out,
        out_shape=[jax.ShapeDtypeStruct((t, d), F32),
                   jax.ShapeDtypeStruct((ng, t, d), BF16),
                   jax.ShapeDtypeStruct((t, LANES), F32)] + cast_shape,
        scratch_shapes=[pltpu.VMEM((n_ec, bm, ec), F32)],
        compiler_params=pltpu.CompilerParams(dimension_semantics=("parallel", "arbitrary"),
                                             vmem_limit_bytes=BIG_VMEM_LIMIT_BYTES),
        name="mlp",
    )(hb, ss, w1, w2, h, gains, *[src for src, _ in cast_jobs])
    return outs[:3], [w[None] for w in outs[3:]]


def _hgrn_kernel(q_ref, f_ref, i_ref, g_ref, lbl_ref, ong_ref, o_ref,
                 st_ref, b_ref, k_ref, qin_ref, qe_ref, ke_ref, kd_ref, dl_ref, gate_ref, oacc_ref, upd_ref,
                 *, layer):
    bt = q_ref.shape[0]
    c_len = HG_CHUNK
    half = c_len // 2
    n_chunks = bt // c_len

    @pl.when(pl.program_id(1) == 0)
    def _():
        st_ref[...] = jnp.zeros_like(st_ref)

    lg = lbl_ref[...]
    e = jnp.exp(lg - jnp.max(lg, axis=0, keepdims=True))
    sm = e / jnp.sum(e, axis=0, keepdims=True)
    cs = sm[0:1]
    for l in range(1, layer + 1):
        cs = cs + sm[l:l + 1]
    lb = cs - sm[0:1]

    sig = 1.0 / (1.0 + jnp.exp(-f_ref[...]))
    fgate = lb + (1.0 - lb) * sig
    kk = 1.0 - fgate
    k_ref[...] = kk
    logf = jnp.log(fgate)
    row = lax.broadcasted_iota(jnp.int32, (c_len, HG_HEAD_DIM), 0)
    shifts = [1 << s for s in range(c_len.bit_length() - 1)]
    keep = [row >= shift for shift in shifts]

    for c in range(n_chunks):
        rows = slice(c * c_len, (c + 1) * c_len)
        bc, kc, q = logf[rows], kk[rows], q_ref[rows, :]
        for shift, mask in zip(shifts, keep):
            bc = bc + jnp.where(mask, pltpu.roll(bc, shift, axis=0), 0.0)
        b_ref[rows, :] = bc
        bl = bc[c_len - 1:c_len, :]
        w = bc - bc[half - 1:half, :]
        qin_ref[rows, :] = (q * jnp.exp(bc)).astype(BF16)
        qe_ref[rows, :] = (q * jnp.exp(w)).astype(BF16)
        ke_ref[rows, :] = (kc * jnp.exp(-w)).astype(BF16)
        kd_ref[rows, :] = (kc * jnp.exp(bl - bc)).astype(BF16)
        dl_ref[c:c + 1, :] = jnp.exp(bl)
    gg = g_ref[...]
    gate_ref[...] = ong_ref[...] * gg / (1.0 + jnp.exp(-gg))

    mids = b_ref[pl.ds(half - 1, bt // half, stride=half), :]
    second = (lax.broadcasted_iota(jnp.int32, mids.shape, 0) & 1) == 1
    safe = jnp.max(jnp.where(second, pltpu.roll(mids, 1, axis=0), 0.0) - mids) <= HG_SAFE_DECAY

    tri = (lax.broadcasted_iota(jnp.int32, (c_len, c_len), 1)
           <= lax.broadcasted_iota(jnp.int32, (c_len, c_len), 0))
    rowc = lax.broadcasted_iota(jnp.int32, (c_len, 1), 0)

    def state_increment(sl):
        return jnp.dot(i_ref[sl, :].T.astype(BF16), kd_ref[sl, :], preferred_element_type=F32)

    def finish():
        o = oacc_ref[...]
        on = o * lax.rsqrt(jnp.mean(o * o, axis=-1, keepdims=True) + NORM_EPS)
        o_ref[...] = (on * gate_ref[...]).astype(BF16)

    @pl.when(safe)
    def _():
        sls = [slice(c * c_len, (c + 1) * c_len) for c in range(n_chunks)]
        atts = [lax.dot_general(qe_ref[sl, :], ke_ref[sl, :], DN_LAST, preferred_element_type=F32)
                for sl in sls]
        atts = [jnp.where(tri, att, 0.0).astype(BF16) for att in atts]
        for sl, att in zip(sls, atts):
            oacc_ref[sl, :] = jnp.dot(att, i_ref[sl, :].astype(BF16), preferred_element_type=F32)
        for c, sl in enumerate(sls):
            upd_ref[c] = state_increment(sl)
        st = st_ref[...]
        for c, sl in enumerate(sls):
            oacc_ref[sl, :] += lax.dot_general(qin_ref[sl, :], st.astype(BF16), DN_LAST,
                                               preferred_element_type=F32)
            st = st * dl_ref[c:c + 1, :] + upd_ref[c]
        st_ref[...] = st
        finish()

    @pl.when(jnp.logical_not(safe))
    def _():
        def body(c, carry):
            r0 = pl.multiple_of(c * c_len, c_len)
            sl = pl.ds(r0, c_len)
            st = st_ref[...]
            q, bc = q_ref[sl, :], b_ref[sl, :]

            def key_row(s, acc):
                bs = b_ref[pl.ds(r0 + s, 1), :]
                ks = k_ref[pl.ds(r0 + s, 1), :]
                vs = i_ref[pl.ds(r0 + s, 1), :]
                a = jnp.sum(q * ks * jnp.exp(jnp.minimum(bc - bs, 0.0)), axis=-1, keepdims=True)
                return acc + jnp.where(rowc >= s, a, 0.0) * vs

            o = lax.dot_general(qin_ref[sl, :], st.astype(BF16), DN_LAST, preferred_element_type=F32)
            oacc_ref[sl, :] = o + lax.fori_loop(0, c_len, key_row, jnp.zeros((c_len, HG_HEAD_DIM), F32))
            st_ref[...] = st * dl_ref[pl.ds(c, 1), :] + state_increment(sl)
            return carry
        lax.fori_loop(0, n_chunks, body, 0)
        finish()


def _hgrn_scan(proj, lb_logits, out_norm_g, layer):
    t = proj.shape[0]
    d = proj.shape[1] // 4
    heads = d // HG_HEAD_DIM
    n_a = lb_logits.shape[0]
    bt = 1024

    def col(part):
        return pl.BlockSpec((bt, HG_HEAD_DIM), lambda h, i: (i, part * heads + h))

    def rows(dtype):
        return pltpu.VMEM((bt, HG_HEAD_DIM), dtype)

    return pl.pallas_call(
        functools.partial(_hgrn_kernel, layer=layer),
        grid=(heads, t // bt),
        in_specs=[col(0), col(1), col(2), col(3),
                  pl.BlockSpec((n_a, HG_HEAD_DIM), lambda h, i: (0, h)),
                  pl.BlockSpec((None, 1, HG_HEAD_DIM), lambda h, i: (layer, 0, h))],
        out_specs=pl.BlockSpec((bt, HG_HEAD_DIM), lambda h, i: (i, h)),
        out_shape=jax.ShapeDtypeStruct((t, d), BF16),
        scratch_shapes=[pltpu.VMEM((HG_HEAD_DIM, HG_HEAD_DIM), F32),
                        rows(F32), rows(F32),
                        rows(BF16), rows(BF16), rows(BF16), rows(BF16),
                        pltpu.VMEM((bt // HG_CHUNK, HG_HEAD_DIM), F32),
                        rows(F32), rows(F32),
                        pltpu.VMEM((bt // HG_CHUNK, HG_HEAD_DIM, HG_HEAD_DIM), F32)],
        compiler_params=_cparams("parallel", "arbitrary"),
        name="hgrn_scan",
    )(proj, proj, proj, proj, lb_logits, out_norm_g.reshape(n_a, 1, d))


def _attn_kernel(sink_ref, bias_ref, q_ref, kc_ref, kp_ref, vc_ref, vp_ref, o_ref):
    jj = pl.program_id(1)
    heads_per_kv = q_ref.shape[1] // (2 * ATT_HEAD_DIM)
    n_slab = heads_per_kv // 2

    k2 = jnp.concatenate([kp_ref[...], kc_ref[...]], axis=0)
    v2 = jnp.concatenate([vp_ref[...], vc_ref[...]], axis=0)
    k2r, v2r = pltpu.roll(k2, ATT_HEAD_DIM, axis=1), pltpu.roll(v2, ATT_HEAD_DIM, axis=1)
    low = lax.broadcasted_iota(jnp.int32, k2.shape, 1) < ATT_HEAD_DIM
    ones_lo = jnp.where(low, 1.0, 0.0)
    real_key = lax.broadcasted_iota(jnp.int32, k2.shape, 0) > 0
    v_low, v_high = low & real_key, jnp.logical_not(low) & real_key
    sink_col = lax.broadcasted_iota(jnp.int32, (WINDOW, LANES), 1) == 0
    bias = bias_ref[...]

    scores, v_exts = [], []
    for kvh in range(2):
        k_lo, k_hi = (k2, k2r) if kvh == 0 else (k2r, k2)
        v_lo, v_hi = (v2, v2r) if kvh == 0 else (v2r, v2)
        k_half = (jnp.where(low, k_lo, 0.0).astype(BF16), jnp.where(low, 0.0, k_hi).astype(BF16))
        v_exts.append((jnp.concatenate([jnp.where(v_low, v_lo, 0.0), ones_lo], axis=1).astype(BF16),
                       jnp.concatenate([jnp.where(v_high, v_hi, 0.0), 1.0 - ones_lo], axis=1).astype(BF16)))
        col0 = kvh * n_slab * LANES
        qs = jnp.concatenate([q_ref[:, col0 + p * LANES:col0 + (p + 1) * LANES] for p in range(n_slab)],
                             axis=0)
        scores.append([lax.dot_general(qs, k_half[parity], DN_LAST, preferred_element_type=F32)
                       for parity in range(2)])

    probs = []
    for kvh in range(2):
        pe = []
        for parity in range(2):
            head0 = (2 * jj + kvh) * heads_per_kv + parity
            slabs = []
            for p in range(n_slab):
                sp = scores[kvh][parity][p * WINDOW:(p + 1) * WINDOW, :] + bias
                sp = jnp.concatenate([jnp.where(sink_col, sink_ref[head0 + 2 * p], sp[:, :LANES]),
                                      sp[:, LANES:]], axis=1)
                slabs.append(jnp.exp(sp - jnp.max(sp, axis=-1, keepdims=True)).astype(BF16))
            pe.append(jnp.concatenate(slabs, axis=0))
        probs.append(pe)

    for kvh in range(2):
        res = [jnp.dot(probs[kvh][parity], v_exts[kvh][parity], preferred_element_type=F32)
               for parity in range(2)]
        out = (res[0][:, :LANES] + res[1][:, :LANES]) / (res[0][:, LANES:] + res[1][:, LANES:])
        col0 = kvh * n_slab * LANES
        for p in range(n_slab):
            o_ref[:, col0 + p * LANES:col0 + (p + 1) * LANES] = out[p * WINDOW:(p + 1) * WINDOW, :].astype(BF16)


def _attention(q, k, v, sinks):
    t, dq = q.shape
    nb = t // WINDOW
    gw = 2 * dq // ATT_KV_HEADS
    a_idx = jnp.arange(WINDOW)[:, None]
    c_idx = jnp.arange(2 * WINDOW)[None, :]
    band = (c_idx >= a_idx + 1) & (c_idx <= a_idx + WINDOW)
    bias = jnp.stack([jnp.where(band & (c_idx >= WINDOW), 0.0, -jnp.inf),
                      jnp.where(band, 0.0, -jnp.inf)]).astype(F32)

    def cur(n, j):
        return (n, j)

    def prev(n, j):
        return (jnp.maximum(n - 1, 0), j)

    return pl.pallas_call(
        _attn_kernel,
        grid=(nb, ATT_KV_HEADS // 2),
        in_specs=[pl.BlockSpec(memory_space=pltpu.SMEM),
                  pl.BlockSpec((None, WINDOW, 2 * WINDOW), lambda n, j: (jnp.minimum(n, 1), 0, 0)),
                  pl.BlockSpec((WINDOW, gw), lambda n, j: (n, j)),
                  pl.BlockSpec((WINDOW, LANES), cur),
                  pl.BlockSpec((WINDOW, LANES), prev),
                  pl.BlockSpec((WINDOW, LANES), cur),
                  pl.BlockSpec((WINDOW, LANES), prev)],
        out_specs=pl.BlockSpec((WINDOW, gw), lambda n, j: (n, j)),
        out_shape=jax.ShapeDtypeStruct((t, dq), BF16),
        compiler_params=_cparams("parallel", "parallel"),
        name="swa_attention",
    )(sinks, bias, q, k, k, v, v)


def kernel(x, p, positions, mixer_norm_g, hgrn_w_in, hgrn_w_out, hgrn_lb_logits, hgrn_out_norm_g,
           kv_norm_g, w_kv, attn_w_q, attn_w_o, attn_sinks, mlp_norm_g, mlp_w1, mlp_w2,
           ple_norm_g, ple_w_gate, ple_w_up, final_norm_g):
    batch, t, d = x.shape
    depth = mlp_w1.shape[0]
    n_a = hgrn_w_in.shape[0]
    assert batch == 1

    w_in, w_kvb, w_up = hgrn_w_in.astype(BF16), w_kv.astype(BF16)[None], ple_w_up.astype(BF16)
    mixer_w = [hgrn_w_out[0:1].astype(BF16)] if n_a > 0 else [attn_w_q[0:1].astype(BF16),
                                                              attn_w_o[0:1].astype(BF16)]
    if n_a == 0:
        w1, w2 = mlp_w1[0:1].astype(BF16), mlp_w2[0:1].astype(BF16)

    cos, sin = _rope_tables(positions)
    h = x.reshape(t, d)
    hb, ss = _prep_stream(h, mixer_norm_g[0])
    k_sh = v_sh = None

    for layer in range(depth):
        mlp_gain = mlp_norm_g[layer][None, :]
        if layer < n_a:
            jobs = [(mlp_w1, 0), (mlp_w2, 0)] if layer == 0 else []
            proj, cast = _proj(hb, ss, 0, w_in, layer, F32, "hgrn_in_proj", jobs)
            if layer == 0:
                w1, w2 = cast
            mixed = _hgrn_scan(proj, hgrn_lb_logits, hgrn_out_norm_g, layer)
            h, hb, ss = _resid_matmul(mixed, mixer_w[0], 0, h, mlp_gain, "hgrn_out_proj")
        else:
            j = layer - n_a
            q = _proj_rope(hb, ss, 0, mixer_w[0], 0, cos, sin, ATT_HEAD_DIM ** -0.5)
            mixed = _attention(q, k_sh, v_sh, attn_sinks[j])
            h, hb, ss = _resid_matmul(mixed, mixer_w[1], 0, h, mlp_gain, "attn_out_proj")

        jobs = [(ple_w_gate, layer)]
        if layer + 1 < depth:
            jobs += [(mlp_w1, layer + 1), (mlp_w2, layer + 1)]
            jobs += ([(hgrn_w_out, layer + 1)] if layer + 1 < n_a
                     else [(attn_w_q, layer + 1 - n_a), (attn_w_o, layer + 1 - n_a)])
        (h, hb, ss), cast = _mlp(hb, ss, w1, w2, 0, h, ple_norm_g[layer][None, :], jobs)
        w_gate = cast[0]
        if layer + 1 < depth:
            w1, w2, mixer_w = cast[1], cast[2], cast[3:]

        if layer + 1 < depth:
            next_gains = [mixer_norm_g[layer + 1]]
            if layer == n_a - 1:
                next_gains.append(kv_norm_g)
        else:
            next_gains = [final_norm_g]
        h, hb, ss = _ple(hb, ss, w_gate, 0, p, w_up, layer, h, jnp.stack(next_gains))

        if layer == n_a - 1:
            k_sh, v_sh = _kv_proj(hb, ss, 1, w_kvb, cos, sin)

    return _final_norm(h, ss, final_norm_g).reshape(batch, t, d)
```

```python
import functools

import jax
import jax.numpy as jnp
from jax import lax
from jax.experimental import pallas as pl
from jax.experimental.pallas import tpu as pltpu

F32 = jnp.float32
BF16 = jnp.bfloat16

LANES = 128
MXU_COLS = 256
VMEM_LIMIT_BYTES = 56 * 1024 * 1024
BIG_VMEM_LIMIT_BYTES = 60 * 1024 * 1024

HG_HEAD_DIM = 128
HG_CHUNK = 64
HG_SAFE_DECAY = 60.0
ATT_HEAD_DIM = 64
ATT_KV_HEADS = 8
WINDOW = 128
ROPE_THETA = 10000.0
NORM_EPS = 1e-6

DN_LAST = (((1,), (1,)), ((), ()))


def _cparams(*sem):
    return pltpu.CompilerParams(dimension_semantics=sem, vmem_limit_bytes=VMEM_LIMIT_BYTES)


def _rstd(ss_ref, d_model):
    return lax.rsqrt(jnp.sum(ss_ref[...], axis=-1, keepdims=True) * (1.0 / d_model) + NORM_EPS)


def _lane_partial_sumsq(x):
    sq = x * x
    acc = sq[:, 0:LANES]
    for c in range(1, x.shape[1] // LANES):
        acc = acc + sq[:, c * LANES:(c + 1) * LANES]
    return acc


def _col_chunks(width):
    return [slice(c, c + MXU_COLS) for c in range(0, width, MXU_COLS)]


def _emit_cols(h_new, cs, g_ref, hn_ref, hb_ref):
    hn_ref[:, cs] = h_new
    for i in range(hb_ref.shape[0]):
        hb_ref[i, :, cs] = (h_new * g_ref[i:i + 1, cs]).astype(BF16)
    return _lane_partial_sumsq(h_new)


def _accumulate_ss(ss_ref, part, first):
    @pl.when(first)
    def _():
        ss_ref[...] = jnp.zeros_like(ss_ref)
    ss_ref[...] += part


def _cast_specs(jobs, n_steps, step_of):
    in_specs, out_specs, out_shape = [], [], []
    for src, layer in jobs:
        _, k, n = src.shape
        rows = k // n_steps
        assert rows * n_steps == k and rows % 16 == 0
        in_specs.append(pl.BlockSpec((None, rows, n), lambda *g, layer=layer: (layer, step_of(*g), 0)))
        out_specs.append(pl.BlockSpec((rows, n), lambda *g: (step_of(*g), 0)))
        out_shape.append(jax.ShapeDtypeStruct((k, n), BF16))
    return in_specs, out_specs, out_shape


def _cast_blocks(src_refs, dst_refs):
    for src, dst in zip(src_refs, dst_refs):
        dst[...] = src[...].astype(BF16)


def _swap_half_heads(x):
    lane = lax.broadcasted_iota(jnp.int32, x.shape, 1)
    first_half = (lane & (ATT_HEAD_DIM - 1)) < (ATT_HEAD_DIM // 2)
    return jnp.where(first_half, pltpu.roll(x, LANES - ATT_HEAD_DIM // 2, axis=1),
                     pltpu.roll(x, ATT_HEAD_DIM // 2, axis=1))


def _rope_store(x, cos, sin_signed, o_ref, col0):
    for c in range(x.shape[1] // LANES):
        xc = x[:, c * LANES:(c + 1) * LANES]
        piece = xc * cos + _swap_half_heads(xc) * sin_signed
        o_ref[:, col0 + c * LANES:col0 + (c + 1) * LANES] = piece.astype(o_ref.dtype)


def _rope_table_kernel(pos_ref, invf_ref, sign_ref, cos_ref, sin_ref):
    ang = pos_ref[...].astype(F32) * invf_ref[...]
    cos_ref[...] = jnp.cos(ang)
    sin_ref[...] = jnp.sin(ang) * sign_ref[...]


def _rope_tables(positions):
    t = positions.shape[-1]
    half = ATT_HEAD_DIM // 2
    inv_freq = ROPE_THETA ** (-jnp.arange(half, dtype=F32) / half)
    invf = jnp.tile(inv_freq, LANES // half)[None, :]
    sign = jnp.tile(jnp.concatenate([-jnp.ones((half,), F32), jnp.ones((half,), F32)]),
                    LANES // ATT_HEAD_DIM)[None, :]
    bt = 1024
    return pl.pallas_call(
        _rope_table_kernel,
        grid=(t // bt,),
        in_specs=[pl.BlockSpec((bt, 1), lambda i: (i, 0)),
                  pl.BlockSpec((1, LANES), lambda i: (0, 0)),
                  pl.BlockSpec((1, LANES), lambda i: (0, 0))],
        out_specs=[pl.BlockSpec((bt, LANES), lambda i: (i, 0))] * 2,
        out_shape=[jax.ShapeDtypeStruct((t, LANES), F32)] * 2,
        compiler_params=_cparams("parallel"),
        name="rope_tables",
    )(positions.reshape(t, 1), invf, sign)


def _prep_kernel(x_ref, g_ref, hb_ref, ss_ref):
    x = x_ref[...]
    hb_ref[0] = (x * g_ref[...]).astype(BF16)
    ss_ref[...] = _lane_partial_sumsq(x)


def _prep_stream(x, g):
    t, d = x.shape
    bt = 256
    return pl.pallas_call(
        _prep_kernel,
        grid=(t // bt,),
        in_specs=[pl.BlockSpec((bt, d), lambda i: (i, 0)),
                  pl.BlockSpec((1, d), lambda i: (0, 0))],
        out_specs=[pl.BlockSpec((1, bt, d), lambda i: (0, i, 0)),
                   pl.BlockSpec((bt, LANES), lambda i: (i, 0))],
        out_shape=[jax.ShapeDtypeStruct((1, t, d), BF16),
                   jax.ShapeDtypeStruct((t, LANES), F32)],
        compiler_params=_cparams("parallel"),
        name="prep_stream",
    )(x, g.reshape(1, d))


def _final_norm_kernel(h_ref, ss_ref, g_ref, o_ref):
    o_ref[...] = h_ref[...] * _rstd(ss_ref, h_ref.shape[1]) * g_ref[...]


def _final_norm(h, ss, g):
    t, d = h.shape
    bt = 256
    return pl.pallas_call(
        _final_norm_kernel,
        grid=(t // bt,),
        in_specs=[pl.BlockSpec((bt, d), lambda i: (i, 0)),
                  pl.BlockSpec((bt, LANES), lambda i: (i, 0)),
                  pl.BlockSpec((1, d), lambda i: (0, 0))],
        out_specs=pl.BlockSpec((bt, d), lambda i: (i, 0)),
        out_shape=jax.ShapeDtypeStruct((t, d), F32),
        compiler_params=_cparams("parallel"),
        name="final_norm",
    )(h, ss, g.reshape(1, d))


def _proj_kernel(*refs, n_cast):
    a_ref, ss_ref, w_ref = refs[:3]
    o_ref = refs[3 + n_cast]
    acc = jnp.dot(a_ref[...], w_ref[...], preferred_element_type=F32)
    o_ref[...] = (acc * _rstd(ss_ref, a_ref.shape[1])).astype(o_ref.dtype)
    _cast_blocks(refs[3:3 + n_cast], refs[4 + n_cast:])


def _proj_rope_kernel(a_ref, ss_ref, w_ref, cos_ref, sin_ref, o_ref, *, scale):
    row_scale = _rstd(ss_ref, a_ref.shape[1]) * scale
    cos, sin = cos_ref[...], sin_ref[...]
    for cs in _col_chunks(o_ref.shape[1]):
        acc = jnp.dot(a_ref[...], w_ref[:, cs], preferred_element_type=F32)
        _rope_store(acc * row_scale, cos, sin, o_ref, cs.start)


def _kv_kernel(a_ref, ss_ref, w_ref, cos_ref, sin_ref, k_ref, v_ref):
    rstd = _rstd(ss_ref, a_ref.shape[1])
    nk = k_ref.shape[1]
    cos, sin = cos_ref[...], sin_ref[...]
    for cs in _col_chunks(nk):
        acc = jnp.dot(a_ref[...], w_ref[:, cs], preferred_element_type=F32)
        _rope_store(acc * rstd, cos, sin, k_ref, cs.start)
    for cs in _col_chunks(nk):
        acc = jnp.dot(a_ref[...], w_ref[:, nk + cs.start:nk + cs.stop], preferred_element_type=F32)
        v_ref[:, cs] = acc * rstd


def _stream_specs(hb, idx, bm, single_buffer=False):
    k = hb.shape[2]
    mode = dict(pipeline_mode=pl.Buffered(1)) if single_buffer else {}
    return [pl.BlockSpec((None, bm, k), lambda m, n: (idx, m, 0), **mode),
            pl.BlockSpec((bm, LANES), lambda m, n: (m, 0))]


def _weight_spec(w, layer, bn):
    return pl.BlockSpec((None, w.shape[1], bn), lambda m, n: (layer, 0, n))


def _proj(hb, ss, idx, w, layer, out_dtype, name, cast_jobs=()):
    _, t, _ = hb.shape
    n = w.shape[2]
    bm, bn = 1024, 1024
    n_col = n // bn
    cast_in, cast_out, cast_shape = _cast_specs(cast_jobs, (t // bm) * n_col, lambda m, n: m * n_col + n)
    outs = pl.pallas_call(
        functools.partial(_proj_kernel, n_cast=len(cast_jobs)),
        grid=(t // bm, n_col),
        in_specs=_stream_specs(hb, idx, bm, bool(cast_jobs)) + [_weight_spec(w, layer, bn)] + cast_in,
        out_specs=[pl.BlockSpec((bm, bn), lambda m, n: (m, n))] + cast_out,
        out_shape=[jax.ShapeDtypeStruct((t, n), out_dtype)] + cast_shape,
        compiler_params=pltpu.CompilerParams(dimension_semantics=("parallel", "parallel"),
                                             vmem_limit_bytes=BIG_VMEM_LIMIT_BYTES),
        name=name,
    )(hb, ss, w, *[src for src, _ in cast_jobs])
    return outs[0], [w[None] for w in outs[1:]]


def _proj_rope(hb, ss, idx, w, layer, cos, sin, scale):
    _, t, _ = hb.shape
    n = w.shape[2]
    bm, bn = 1024, 1024
    return pl.pallas_call(
        functools.partial(_proj_rope_kernel, scale=scale),
        grid=(t // bm, n // bn),
        in_specs=_stream_specs(hb, idx, bm) + [
            _weight_spec(w, layer, bn),
            pl.BlockSpec((bm, LANES), lambda m, n: (m, 0)),
            pl.BlockSpec((bm, LANES), lambda m, n: (m, 0))],
        out_specs=pl.BlockSpec((bm, bn), lambda m, n: (m, n)),
        out_shape=jax.ShapeDtypeStruct((t, n), BF16),
        compiler_params=_cparams("parallel", "parallel"),
        name="q_proj_rope",
    )(hb, ss, w, cos, sin)


def _kv_proj(hb, ss, idx, w, cos, sin):
    _, t, _ = hb.shape
    n = w.shape[2]
    nk = n // 2
    bm = 1024
    return pl.pallas_call(
        _kv_kernel,
        grid=(t // bm, 1),
        in_specs=_stream_specs(hb, idx, bm) + [
            _weight_spec(w, 0, n),
            pl.BlockSpec((bm, LANES), lambda m, n: (m, 0)),
            pl.BlockSpec((bm, LANES), lambda m, n: (m, 0))],
        out_specs=[pl.BlockSpec((bm, nk), lambda m, n: (m, 0))] * 2,
        out_shape=[jax.ShapeDtypeStruct((t, nk), F32)] * 2,
        compiler_params=_cparams("parallel", "arbitrary"),
        name="kv_proj_rope",
    )(hb, ss, w, cos, sin)


def _resid_kernel(a_ref, w_ref, h_ref, g_ref, hn_ref, hb_ref, ss_ref):
    part = None
    for cs in _col_chunks(hn_ref.shape[1]):
        acc = jnp.dot(a_ref[...], w_ref[:, cs], preferred_element_type=F32)
        p = _emit_cols(h_ref[:, cs] + acc, cs, g_ref, hn_ref, hb_ref)
        part = p if part is None else part + p
    _accumulate_ss(ss_ref, part, pl.program_id(1) == 0)


def _ple_kernel(a_ref, ssin_ref, w_ref, p_ref, wup_ref, h_ref, g_ref, hn_ref, hb_ref, ss_ref):
    rstd = _rstd(ssin_ref, a_ref.shape[1])
    pb = p_ref[...].astype(BF16)
    part = None
    for cs in _col_chunks(hn_ref.shape[1]):
        z = jnp.dot(a_ref[...], w_ref[:, cs], preferred_element_type=F32) * rstd
        gate = 1.0 / (1.0 + jnp.exp(-z))
        up = jnp.dot(pb, wup_ref[:, cs], preferred_element_type=F32)
        p = _emit_cols(h_ref[:, cs] + gate * up, cs, g_ref, hn_ref, hb_ref)
        part = p if part is None else part + p
    _accumulate_ss(ss_ref, part, pl.program_id(1) == 0)


def _stream_out(t, d, ng, bm, bn):
    specs = [pl.BlockSpec((bm, bn), lambda m, n: (m, n)),
             pl.BlockSpec((ng, bm, bn), lambda m, n: (0, m, n)),
             pl.BlockSpec((bm, LANES), lambda m, n: (m, 0))]
    shapes = [jax.ShapeDtypeStruct((t, d), F32),
              jax.ShapeDtypeStruct((ng, t, d), BF16),
              jax.ShapeDtypeStruct((t, LANES), F32)]
    return specs, shapes


def _resid_matmul(a, w, layer, h, gains, name):
    t, k = a.shape
    d = w.shape[2]
    ng = gains.shape[0]
    bm, bn = 1024, 1024
    out_specs, out_shape = _stream_out(t, d, ng, bm, bn)
    return pl.pallas_call(
        _resid_kernel,
        grid=(t // bm, d // bn),
        in_specs=[pl.BlockSpec((bm, k), lambda m, n: (m, 0), pipeline_mode=pl.Buffered(1)),
                  _weight_spec(w, layer, bn),
                  pl.BlockSpec((bm, bn), lambda m, n: (m, n)),
                  pl.BlockSpec((ng, bn), lambda m, n: (0, n))],
        out_specs=out_specs,
        out_shape=out_shape,
        compiler_params=_cparams("parallel", "arbitrary"),
        name=name,
    )(a, w, h, gains)


def _ple(hb, ss, w_gate, gate_layer, p, w_up, layer, h, gains):
    _, t, _ = hb.shape
    d = w_gate.shape[2]
    ng = gains.shape[0]
    pdim = p.shape[-1]
    bm, bn = 1024, 1024
    out_specs, out_shape = _stream_out(t, d, ng, bm, bn)
    return pl.pallas_call(
        _ple_kernel,
        grid=(t // bm, d // bn),
        in_specs=_stream_specs(hb, 0, bm, True) + [
            _weight_spec(w_gate, gate_layer, bn),
            pl.BlockSpec((None, None, bm, pdim), lambda m, n: (layer, 0, m, 0)),
            _weight_spec(w_up, layer, bn),
            pl.BlockSpec((bm, bn), lambda m, n: (m, n)),
            pl.BlockSpec((ng, bn), lambda m, n: (0, n))],
        out_specs=out_specs,
        out_shape=out_shape,
        compiler_params=pltpu.CompilerParams(dimension_semantics=("parallel", "arbitrary"),
                                             vmem_limit_bytes=BIG_VMEM_LIMIT_BYTES),
        name="ple",
    )(hb, ss, w_gate, p, w_up, h, gains)


def _mlp_kernel(*refs, n_ff, n_cast):
    a_ref, ssin_ref, w1_ref, w2_ref, h_ref, g_ref = refs[:6]
    cast_src = refs[6:6 + n_cast]
    hn_ref, hb_ref, ss_ref = refs[6 + n_cast:9 + n_cast]
    cast_dst = refs[9 + n_cast:9 + 2 * n_cast]
    acc_ref = refs[-1]
    f = pl.program_id(1)
    ec = acc_ref.shape[2]

    @pl.when(f == 0)
    def _():
        acc_ref[...] = jnp.zeros_like(acc_ref)

    @pl.when(f < n_ff)
    def _():
        z = jnp.dot(a_ref[...], w1_ref[...], preferred_element_type=F32)
        z = jnp.maximum(z * _rstd(ssin_ref, a_ref.shape[1]), 0.0)
        act = (z * z).astype(BF16)
        for j in range(acc_ref.shape[0]):
            acc_ref[j] += jnp.dot(act, w2_ref[:, j * ec:(j + 1) * ec], preferred_element_type=F32)
        _cast_blocks(cast_src, cast_dst)

    @pl.when(f >= n_ff)
    def _():
        part = _emit_cols(h_ref[...] + acc_ref[f - n_ff], slice(0, ec), g_ref, hn_ref, hb_ref)
        _accumulate_ss(ss_ref, part, f == n_ff)


def _mlp(hb, ss, w1, w2, layer, h, gains, cast_jobs=()):
    _, t, k = hb.shape
    d_ff = w1.shape[2]
    d = w2.shape[2]
    ng = gains.shape[0]
    bm, bf, ec = 1024, 512, 256
    n_ff, n_ec = d_ff // bf, d // ec
    last = n_ff - 1

    def epi(m, f):
        return (m, jnp.maximum(f - n_ff, 0))

    cast_in, cast_out, cast_shape = _cast_specs(cast_jobs, (t // bm) * n_ff,
                                                lambda m, f: m * n_ff + jnp.minimum(f, last))
    outs = pl.pallas_call(
        functools.partial(_mlp_kernel, n_ff=n_ff, n_cast=len(cast_jobs)),
        grid=(t // bm, n_ff + n_ec),
        in_specs=[
            pl.BlockSpec((None, bm, k), lambda m, f: (0, m, 0), pipeline_mode=pl.Buffered(1)),
            pl.BlockSpec((bm, LANES), lambda m, f: (m, 0)),
            pl.BlockSpec((None, k, bf), lambda m, f: (layer, 0, jnp.minimum(f, last))),
            pl.BlockSpec((None, bf, d), lambda m, f: (layer, jnp.minimum(f, last), 0)),
            pl.BlockSpec((bm, ec), epi),
            pl.BlockSpec((ng, ec), lambda m, f: (0, jnp.maximum(f - n_ff, 0)))] + cast_in,
        out_specs=[pl.BlockSpec((bm, ec), epi),
                   pl.BlockSpec((ng, bm, ec), lambda m, f: (0, m, jnp.maximum(f - n_ff, 0))),
                   pl.BlockSpec((bm, LANES), lambda m, f: (m, 0))] + cast_out,
        out_shape=[jax.ShapeDtypeStruct((t, d), F32),
                   jax.ShapeDtypeStruct((ng, t, d), BF16),
                   jax.ShapeDtypeStruct((t, LANES), F32)] + cast_shape,
        scratch_shapes=[pltpu.VMEM((n_ec, bm, ec), F32)],
        compiler_params=pltpu.CompilerParams(dimension_semantics=("parallel", "arbitrary"),
                                             vmem_limit_bytes=BIG_VMEM_LIMIT_BYTES),
        name="mlp",
    )(hb, ss, w1, w2, h, gains, *[src for src, _ in cast_jobs])
    return outs[:3], [w[None] for w in outs[3:]]


def _hgrn_kernel(*refs, layer, n_cast):
    q_ref, f_ref, i_ref, g_ref, lbl_ref, ong_ref = refs[:6]
    o_ref = refs[6 + n_cast]
    (st_ref, b_ref, k_ref, qin_ref, qe_ref, ke_ref, kd_ref, dl_ref, gate_ref, oacc_ref,
     upd_ref) = refs[7 + 2 * n_cast:]
    _cast_blocks(refs[6:6 + n_cast], refs[7 + n_cast:7 + 2 * n_cast])
    _hgrn_body(q_ref, f_ref, i_ref, g_ref, lbl_ref, ong_ref, o_ref, st_ref, b_ref, k_ref, qin_ref,
               qe_ref, ke_ref, kd_ref, dl_ref, gate_ref, oacc_ref, upd_ref, layer)


def _hgrn_body(q_ref, f_ref, i_ref, g_ref, lbl_ref, ong_ref, o_ref,
               st_ref, b_ref, k_ref, qin_ref, qe_ref, ke_ref, kd_ref, dl_ref, gate_ref, oacc_ref, upd_ref,
               layer):
    bt = q_ref.shape[0]
    c_len = HG_CHUNK
    half = c_len // 2
    n_chunks = bt // c_len

    @pl.when(pl.program_id(1) == 0)
    def _():
        st_ref[...] = jnp.zeros_like(st_ref)

    lg = lbl_ref[...]
    e = jnp.exp(lg - jnp.max(lg, axis=0, keepdims=True))
    sm = e / jnp.sum(e, axis=0, keepdims=True)
    cs = sm[0:1]
    for l in range(1, layer + 1):
        cs = cs + sm[l:l + 1]
    lb = cs - sm[0:1]

    sig = 1.0 / (1.0 + jnp.exp(-f_ref[...]))
    fgate = lb + (1.0 - lb) * sig
    kk = 1.0 - fgate
    k_ref[...] = kk
    logf = jnp.log(fgate)
    row = lax.broadcasted_iota(jnp.int32, (c_len, HG_HEAD_DIM), 0)
    shifts = [1 << s for s in range(c_len.bit_length() - 1)]
    keep = [row >= shift for shift in shifts]

    for c in range(n_chunks):
        rows = slice(c * c_len, (c + 1) * c_len)
        bc, kc, q = logf[rows], kk[rows], q_ref[rows, :]
        for shift, mask in zip(shifts, keep):
            bc = bc + jnp.where(mask, pltpu.roll(bc, shift, axis=0), 0.0)
        b_ref[rows, :] = bc
        bl = bc[c_len - 1:c_len, :]
        w = bc - bc[half - 1:half, :]
        qin_ref[rows, :] = (q * jnp.exp(bc)).astype(BF16)
        qe_ref[rows, :] = (q * jnp.exp(w)).astype(BF16)
        ke_ref[rows, :] = (kc * jnp.exp(-w)).astype(BF16)
        kd_ref[rows, :] = (kc * jnp.exp(bl - bc)).astype(BF16)
        dl_ref[c:c + 1, :] = jnp.exp(bl)
    gg = g_ref[...]
    gate_ref[...] = ong_ref[...] * gg / (1.0 + jnp.exp(-gg))

    mids = b_ref[pl.ds(half - 1, bt // half, stride=half), :]
    second = (lax.broadcasted_iota(jnp.int32, mids.shape, 0) & 1) == 1
    safe = jnp.max(jnp.where(second, pltpu.roll(mids, 1, axis=0), 0.0) - mids) <= HG_SAFE_DECAY

    tri = (lax.broadcasted_iota(jnp.int32, (c_len, c_len), 1)
           <= lax.broadcasted_iota(jnp.int32, (c_len, c_len), 0))
    rowc = lax.broadcasted_iota(jnp.int32, (c_len, 1), 0)

    def state_increment(sl):
        return jnp.dot(i_ref[sl, :].T.astype(BF16), kd_ref[sl, :], preferred_element_type=F32)

    def finish():
        o = oacc_ref[...]
        on = o * lax.rsqrt(jnp.mean(o * o, axis=-1, keepdims=True) + NORM_EPS)
        o_ref[...] = (on * gate_ref[...]).astype(BF16)

    @pl.when(safe)
    def _():
        sls = [slice(c * c_len, (c + 1) * c_len) for c in range(n_chunks)]
        atts = [lax.dot_general(qe_ref[sl, :], ke_ref[sl, :], DN_LAST, preferred_element_type=F32)
                for sl in sls]
        atts = [jnp.where(tri, att, 0.0).astype(BF16) for att in atts]
        for sl, att in zip(sls, atts):
            oacc_ref[sl, :] = jnp.dot(att, i_ref[sl, :].astype(BF16), preferred_element_type=F32)
        for c, sl in enumerate(sls):
            upd_ref[c] = state_increment(sl)
        st = st_ref[...]
        for c, sl in enumerate(sls):
            oacc_ref[sl, :] += lax.dot_general(qin_ref[sl, :], st.astype(BF16), DN_LAST,
                                               preferred_element_type=F32)
            st = st * dl_ref[c:c + 1, :] + upd_ref[c]
        st_ref[...] = st
        finish()

    @pl.when(jnp.logical_not(safe))
    def _():
        def body(c, carry):
            r0 = pl.multiple_of(c * c_len, c_len)
            sl = pl.ds(r0, c_len)
            st = st_ref[...]
            q, bc = q_ref[sl, :], b_ref[sl, :]

            def key_row(s, acc):
                bs = b_ref[pl.ds(r0 + s, 1), :]
                ks = k_ref[pl.ds(r0 + s, 1), :]
                vs = i_ref[pl.ds(r0 + s, 1), :]
                a = jnp.sum(q * ks * jnp.exp(jnp.minimum(bc - bs, 0.0)), axis=-1, keepdims=True)
                return acc + jnp.where(rowc >= s, a, 0.0) * vs

            o = lax.dot_general(qin_ref[sl, :], st.astype(BF16), DN_LAST, preferred_element_type=F32)
            oacc_ref[sl, :] = o + lax.fori_loop(0, c_len, key_row, jnp.zeros((c_len, HG_HEAD_DIM), F32))
            st_ref[...] = st * dl_ref[pl.ds(c, 1), :] + state_increment(sl)
            return carry
        lax.fori_loop(0, n_chunks, body, 0)
        finish()


def _hgrn_scan(proj, lb_logits, out_norm_g, layer, cast_jobs=()):
    t = proj.shape[0]
    d = proj.shape[1] // 4
    heads = d // HG_HEAD_DIM
    n_a = lb_logits.shape[0]
    bt = 1024

    def col(part):
        return pl.BlockSpec((bt, HG_HEAD_DIM), lambda h, i: (i, part * heads + h))

    def rows(dtype):
        return pltpu.VMEM((bt, HG_HEAD_DIM), dtype)

    n_row = t // bt
    cast_in, cast_out, cast_shape = _cast_specs(cast_jobs, heads * n_row, lambda h, i: h * n_row + i)
    outs = pl.pallas_call(
        functools.partial(_hgrn_kernel, layer=layer, n_cast=len(cast_jobs)),
        grid=(heads, n_row),
        in_specs=[col(0), col(1), col(2), col(3),
                  pl.BlockSpec((n_a, HG_HEAD_DIM), lambda h, i: (0, h)),
                  pl.BlockSpec((None, 1, HG_HEAD_DIM), lambda h, i: (layer, 0, h))] + cast_in,
        out_specs=[pl.BlockSpec((bt, HG_HEAD_DIM), lambda h, i: (i, h))] + cast_out,
        out_shape=[jax.ShapeDtypeStruct((t, d), BF16)] + cast_shape,
        scratch_shapes=[pltpu.VMEM((HG_HEAD_DIM, HG_HEAD_DIM), F32),
                        rows(F32), rows(F32),
                        rows(BF16), rows(BF16), rows(BF16), rows(BF16),
                        pltpu.VMEM((bt // HG_CHUNK, HG_HEAD_DIM), F32),
                        rows(F32), rows(F32),
                        pltpu.VMEM((bt // HG_CHUNK, HG_HEAD_DIM, HG_HEAD_DIM), F32)],
        compiler_params=_cparams("parallel", "arbitrary"),
        name="hgrn_scan",
    )(proj, proj, proj, proj, lb_logits, out_norm_g.reshape(n_a, 1, d), *[src for src, _ in cast_jobs])
    return outs[0], [w[None] for w in outs[1:]]


def _attn_kernel(sink_ref, bias_ref, q_ref, kc_ref, kp_ref, vc_ref, vp_ref, o_ref):
    jj = pl.program_id(1)
    heads_per_kv = q_ref.shape[1] // (2 * ATT_HEAD_DIM)
    n_slab = heads_per_kv // 2

    k2 = jnp.concatenate([kp_ref[...], kc_ref[...]], axis=0)
    v2 = jnp.concatenate([vp_ref[...], vc_ref[...]], axis=0)
    k2r, v2r = pltpu.roll(k2, ATT_HEAD_DIM, axis=1), pltpu.roll(v2, ATT_HEAD_DIM, axis=1)
    low = lax.broadcasted_iota(jnp.int32, k2.shape, 1) < ATT_HEAD_DIM
    ones_lo = jnp.where(low, 1.0, 0.0)
    real_key = lax.broadcasted_iota(jnp.int32, k2.shape, 0) > 0
    v_low, v_high = low & real_key, jnp.logical_not(low) & real_key
    sink_col = lax.broadcasted_iota(jnp.int32, (WINDOW, LANES), 1) == 0
    bias = bias_ref[...]

    scores, v_exts = [], []
    for kvh in range(2):
        k_lo, k_hi = (k2, k2r) if kvh == 0 else (k2r, k2)
        v_lo, v_hi = (v2, v2r) if kvh == 0 else (v2r, v2)
        k_half = (jnp.where(low, k_lo, 0.0).astype(BF16), jnp.where(low, 0.0, k_hi).astype(BF16))
        v_exts.append((jnp.concatenate([jnp.where(v_low, v_lo, 0.0), ones_lo], axis=1).astype(BF16),
                       jnp.concatenate([jnp.where(v_high, v_hi, 0.0), 1.0 - ones_lo], axis=1).astype(BF16)))
        col0 = kvh * n_slab * LANES
        qs = jnp.concatenate([q_ref[:, col0 + p * LANES:col0 + (p + 1) * LANES] for p in range(n_slab)],
                             axis=0)
        scores.append([lax.dot_general(qs, k_half[parity], DN_LAST, preferred_element_type=F32)
                       for parity in range(2)])

    probs = []
    for kvh in range(2):
        pe = []
        for parity in range(2):
            head0 = (2 * jj + kvh) * heads_per_kv + parity
            slabs = []
            for p in range(n_slab):
                sp = scores[kvh][parity][p * WINDOW:(p + 1) * WINDOW, :] + bias
                sp = jnp.concatenate([jnp.where(sink_col, sink_ref[head0 + 2 * p], sp[:, :LANES]),
                                      sp[:, LANES:]], axis=1)
                slabs.append(jnp.exp(sp - jnp.max(sp, axis=-1, keepdims=True)).astype(BF16))
            pe.append(jnp.concatenate(slabs, axis=0))
        probs.append(pe)

    for kvh in range(2):
        res = [jnp.dot(probs[kvh][parity], v_exts[kvh][parity], preferred_element_type=F32)
               for parity in range(2)]
        out = (res[0][:, :LANES] + res[1][:, :LANES]) / (res[0][:, LANES:] + res[1][:, LANES:])
        col0 = kvh * n_slab * LANES
        for p in range(n_slab):
            o_ref[:, col0 + p * LANES:col0 + (p + 1) * LANES] = out[p * WINDOW:(p + 1) * WINDOW, :].astype(BF16)


def _attention(q, k, v, sinks):
    t, dq = q.shape
    nb = t // WINDOW
    gw = 2 * dq // ATT_KV_HEADS
    a_idx = jnp.arange(WINDOW)[:, None]
    c_idx = jnp.arange(2 * WINDOW)[None, :]
    band = (c_idx >= a_idx + 1) & (c_idx <= a_idx + WINDOW)
    bias = jnp.stack([jnp.where(band & (c_idx >= WINDOW), 0.0, -jnp.inf),
                      jnp.where(band, 0.0, -jnp.inf)]).astype(F32)

    def cur(n, j):
        return (n, j)

    def prev(n, j):
        return (jnp.maximum(n - 1, 0), j)

    return pl.pallas_call(
        _attn_kernel,
        grid=(nb, ATT_KV_HEADS // 2),
        in_specs=[pl.BlockSpec(memory_space=pltpu.SMEM),
                  pl.BlockSpec((None, WINDOW, 2 * WINDOW), lambda n, j: (jnp.minimum(n, 1), 0, 0)),
                  pl.BlockSpec((WINDOW, gw), lambda n, j: (n, j)),
                  pl.BlockSpec((WINDOW, LANES), cur),
                  pl.BlockSpec((WINDOW, LANES), prev),
                  pl.BlockSpec((WINDOW, LANES), cur),
                  pl.BlockSpec((WINDOW, LANES), prev)],
        out_specs=pl.BlockSpec((WINDOW, gw), lambda n, j: (n, j)),
        out_shape=jax.ShapeDtypeStruct((t, dq), BF16),
        compiler_params=_cparams("parallel", "parallel"),
        name="swa_attention",
    )(sinks, bias, q, k, k, v, v)


def kernel(x, p, positions, mixer_norm_g, hgrn_w_in, hgrn_w_out, hgrn_lb_logits, hgrn_out_norm_g,
           kv_norm_g, w_kv, attn_w_q, attn_w_o, attn_sinks, mlp_norm_g, mlp_w1, mlp_w2,
           ple_norm_g, ple_w_gate, ple_w_up, final_norm_g):
    batch, t, d = x.shape
    depth = mlp_w1.shape[0]
    n_a = hgrn_w_in.shape[0]
    assert batch == 1

    w_in, w_kvb, w_up = hgrn_w_in[0:1].astype(BF16), w_kv.astype(BF16)[None], ple_w_up.astype(BF16)
    mixer_w = [hgrn_w_out[0:1].astype(BF16)] if n_a > 0 else [attn_w_q[0:1].astype(BF16),
                                                              attn_w_o[0:1].astype(BF16)]
    if n_a == 0:
        w1, w2 = mlp_w1[0:1].astype(BF16), mlp_w2[0:1].astype(BF16)

    cos, sin = _rope_tables(positions)
    h = x.reshape(t, d)
    hb, ss = _prep_stream(h, mixer_norm_g[0])
    k_sh = v_sh = None

    for layer in range(depth):
        mlp_gain = mlp_norm_g[layer][None, :]
        if layer < n_a:
            jobs = [(mlp_w1, 0), (mlp_w2, 0)] if layer == 0 else []
            proj, cast = _proj(hb, ss, 0, w_in, 0, F32, "hgrn_in_proj", jobs)
            if layer == 0:
                w1, w2 = cast
            jobs = [(hgrn_w_in, layer + 1)] if layer + 1 < n_a else []
            mixed, cast = _hgrn_scan(proj, hgrn_lb_logits, hgrn_out_norm_g, layer, jobs)
            if jobs:
                w_in = cast[0]
            h, hb, ss = _resid_matmul(mixed, mixer_w[0], 0, h, mlp_gain, "hgrn_out_proj")
        else:
            j = layer - n_a
            q = _proj_rope(hb, ss, 0, mixer_w[0], 0, cos, sin, ATT_HEAD_DIM ** -0.5)
            mixed = _attention(q, k_sh, v_sh, attn_sinks[j])
            h, hb, ss = _resid_matmul(mixed, mixer_w[1], 0, h, mlp_gain, "attn_out_proj")

        jobs = [(ple_w_gate, layer)]
        if layer + 1 < depth:
            jobs += [(mlp_w1, layer + 1), (mlp_w2, layer + 1)]
            jobs += ([(hgrn_w_out, layer + 1)] if layer + 1 < n_a
                     else [(attn_w_q, layer + 1 - n_a), (attn_w_o, layer + 1 - n_a)])
        (h, hb, ss), cast = _mlp(hb, ss, w1, w2, 0, h, ple_norm_g[layer][None, :], jobs)
        w_gate = cast[0]
        if layer + 1 < depth:
            w1, w2, mixer_w = cast[1], cast[2], cast[3:]

        if layer + 1 < depth:
            next_gains = [mixer_norm_g[layer + 1]]
            if layer == n_a - 1:
                next_gains.append(kv_norm_g)
        else:
            next_gains = [final_norm_g]
        h, hb, ss = _ple(hb, ss, w_gate, 0, p, w_up, layer, h, jnp.stack(next_gains))

        if layer == n_a - 1:
            k_sh, v_sh = _kv_proj(hb, ss, 1, w_kvb, cos, sin)

    return _final_norm(h, ss, final_norm_g).reshape(batch, t, d)
```

```python
import functools

import jax
import jax.numpy as jnp
from jax import lax
from jax.experimental import pallas as pl
from jax.experimental.pallas import tpu as pltpu

F32 = jnp.float32
BF16 = jnp.bfloat16

LANES = 128
MXU_COLS = 256
VMEM_LIMIT_BYTES = 56 * 1024 * 1024
BIG_VMEM_LIMIT_BYTES = 60 * 1024 * 1024

HG_HEAD_DIM = 128
HG_CHUNK = 64
HG_SAFE_DECAY = 60.0
ATT_HEAD_DIM = 64
ATT_KV_HEADS = 8
WINDOW = 128
ROPE_THETA = 10000.0
NORM_EPS = 1e-6

DN_LAST = (((1,), (1,)), ((), ()))


def _cparams(*sem):
    return pltpu.CompilerParams(dimension_semantics=sem, vmem_limit_bytes=VMEM_LIMIT_BYTES)


def _rstd(ss_ref, d_model):
    return lax.rsqrt(jnp.sum(ss_ref[...], axis=-1, keepdims=True) * (1.0 / d_model) + NORM_EPS)


def _lane_partial_sumsq(x):
    sq = x * x
    acc = sq[:, 0:LANES]
    for c in range(1, x.shape[1] // LANES):
        acc = acc + sq[:, c * LANES:(c + 1) * LANES]
    return acc


def _col_chunks(width):
    return [slice(c, c + MXU_COLS) for c in range(0, width, MXU_COLS)]


def _emit_cols(h_new, cs, g_ref, hn_ref, hb_ref):
    hn_ref[:, cs] = h_new
    for i in range(hb_ref.shape[0]):
        hb_ref[i, :, cs] = (h_new * g_ref[i:i + 1, cs]).astype(BF16)
    return _lane_partial_sumsq(h_new)


def _accumulate_ss(ss_ref, part, first):
    @pl.when(first)
    def _():
        ss_ref[...] = jnp.zeros_like(ss_ref)
    ss_ref[...] += part


def _cast_specs(jobs, n_steps, step_of):
    in_specs, out_specs, out_shape = [], [], []
    for src, layer in jobs:
        _, k, n = src.shape
        rows = k // n_steps
        assert rows * n_steps == k and rows % 16 == 0
        in_specs.append(pl.BlockSpec((None, rows, n), lambda *g, layer=layer: (layer, step_of(*g), 0)))
        out_specs.append(pl.BlockSpec((rows, n), lambda *g: (step_of(*g), 0)))
        out_shape.append(jax.ShapeDtypeStruct((k, n), BF16))
    return in_specs, out_specs, out_shape


def _cast_blocks(src_refs, dst_refs):
    for src, dst in zip(src_refs, dst_refs):
        dst[...] = src[...].astype(BF16)


def _swap_half_heads(x):
    lane = lax.broadcasted_iota(jnp.int32, x.shape, 1)
    first_half = (lane & (ATT_HEAD_DIM - 1)) < (ATT_HEAD_DIM // 2)
    return jnp.where(first_half, pltpu.roll(x, LANES - ATT_HEAD_DIM // 2, axis=1),
                     pltpu.roll(x, ATT_HEAD_DIM // 2, axis=1))


def _rope_store(x, cos, sin_signed, o_ref, col0):
    for c in range(x.shape[1] // LANES):
        xc = x[:, c * LANES:(c + 1) * LANES]
        piece = xc * cos + _swap_half_heads(xc) * sin_signed
        o_ref[:, col0 + c * LANES:col0 + (c + 1) * LANES] = piece.astype(o_ref.dtype)


def _rope_table_kernel(pos_ref, invf_ref, sign_ref, cos_ref, sin_ref):
    ang = pos_ref[...].astype(F32) * invf_ref[...]
    cos_ref[...] = jnp.cos(ang)
    sin_ref[...] = jnp.sin(ang) * sign_ref[...]


def _rope_tables(positions):
    t = positions.shape[-1]
    half = ATT_HEAD_DIM // 2
    inv_freq = ROPE_THETA ** (-jnp.arange(half, dtype=F32) / half)
    invf = jnp.tile(inv_freq, LANES // half)[None, :]
    sign = jnp.tile(jnp.concatenate([-jnp.ones((half,), F32), jnp.ones((half,), F32)]),
                    LANES // ATT_HEAD_DIM)[None, :]
    bt = 1024
    return pl.pallas_call(
        _rope_table_kernel,
        grid=(t // bt,),
        in_specs=[pl.BlockSpec((bt, 1), lambda i: (i, 0)),
                  pl.BlockSpec((1, LANES), lambda i: (0, 0)),
                  pl.BlockSpec((1, LANES), lambda i: (0, 0))],
        out_specs=[pl.BlockSpec((bt, LANES), lambda i: (i, 0))] * 2,
        out_shape=[jax.ShapeDtypeStruct((t, LANES), F32)] * 2,
        compiler_params=_cparams("parallel"),
        name="rope_tables",
    )(positions.reshape(t, 1), invf, sign)


def _prep_kernel(x_ref, g_ref, hb_ref, ss_ref):
    x = x_ref[...]
    hb_ref[0] = (x * g_ref[...]).astype(BF16)
    ss_ref[...] = _lane_partial_sumsq(x)


def _prep_stream(x, g):
    t, d = x.shape
    bt = 256
    return pl.pallas_call(
        _prep_kernel,
        grid=(t // bt,),
        in_specs=[pl.BlockSpec((bt, d), lambda i: (i, 0)),
                  pl.BlockSpec((1, d), lambda i: (0, 0))],
        out_specs=[pl.BlockSpec((1, bt, d), lambda i: (0, i, 0)),
                   pl.BlockSpec((bt, LANES), lambda i: (i, 0))],
        out_shape=[jax.ShapeDtypeStruct((1, t, d), BF16),
                   jax.ShapeDtypeStruct((t, LANES), F32)],
        compiler_params=_cparams("parallel"),
        name="prep_stream",
    )(x, g.reshape(1, d))


def _final_norm_kernel(h_ref, ss_ref, g_ref, o_ref):
    o_ref[...] = h_ref[...] * _rstd(ss_ref, h_ref.shape[1]) * g_ref[...]


def _final_norm(h, ss, g):
    t, d = h.shape
    bt = 256
    return pl.pallas_call(
        _final_norm_kernel,
        grid=(t // bt,),
        in_specs=[pl.BlockSpec((bt, d), lambda i: (i, 0)),
                  pl.BlockSpec((bt, LANES), lambda i: (i, 0)),
                  pl.BlockSpec((1, d), lambda i: (0, 0))],
        out_specs=pl.BlockSpec((bt, d), lambda i: (i, 0)),
        out_shape=jax.ShapeDtypeStruct((t, d), F32),
        compiler_params=_cparams("parallel"),
        name="final_norm",
    )(h, ss, g.reshape(1, d))


def _proj_kernel(*refs, n_cast):
    a_ref, ss_ref, w_ref = refs[:3]
    o_ref = refs[3 + n_cast]
    acc = jnp.dot(a_ref[...], w_ref[...], preferred_element_type=F32)
    o_ref[...] = (acc * _rstd(ss_ref, a_ref.shape[1])).astype(o_ref.dtype)
    _cast_blocks(refs[3:3 + n_cast], refs[4 + n_cast:])


def _proj_rope_kernel(a_ref, ss_ref, w_ref, cos_ref, sin_ref, o_ref, *, scale):
    row_scale = _rstd(ss_ref, a_ref.shape[1]) * scale
    cos, sin = cos_ref[...], sin_ref[...]
    for cs in _col_chunks(o_ref.shape[1]):
        acc = jnp.dot(a_ref[...], w_ref[:, cs], preferred_element_type=F32)
        _rope_store(acc * row_scale, cos, sin, o_ref, cs.start)


def _kv_kernel(a_ref, ss_ref, w_ref, cos_ref, sin_ref, k_ref, v_ref):
    rstd = _rstd(ss_ref, a_ref.shape[1])
    nk = k_ref.shape[1]
    cos, sin = cos_ref[...], sin_ref[...]
    for cs in _col_chunks(nk):
        acc = jnp.dot(a_ref[...], w_ref[:, cs], preferred_element_type=F32)
        _rope_store(acc * rstd, cos, sin, k_ref, cs.start)
    for cs in _col_chunks(nk):
        acc = jnp.dot(a_ref[...], w_ref[:, nk + cs.start:nk + cs.stop], preferred_element_type=F32)
        v_ref[:, cs] = acc * rstd


def _stream_specs(hb, idx, bm, single_buffer=False):
    k = hb.shape[2]
    mode = dict(pipeline_mode=pl.Buffered(1)) if single_buffer else {}
    return [pl.BlockSpec((None, bm, k), lambda m, n: (idx, m, 0), **mode),
            pl.BlockSpec((bm, LANES), lambda m, n: (m, 0))]


def _weight_spec(w, layer, bn):
    return pl.BlockSpec((None, w.shape[1], bn), lambda m, n: (layer, 0, n))


def _proj(hb, ss, idx, w, layer, out_dtype, name, cast_jobs=()):
    _, t, _ = hb.shape
    n = w.shape[2]
    bm, bn = 1024, 1024
    n_col = n // bn
    cast_in, cast_out, cast_shape = _cast_specs(cast_jobs, (t // bm) * n_col, lambda m, n: m * n_col + n)
    outs = pl.pallas_call(
        functools.partial(_proj_kernel, n_cast=len(cast_jobs)),
        grid=(t // bm, n_col),
        in_specs=_stream_specs(hb, idx, bm, bool(cast_jobs)) + [_weight_spec(w, layer, bn)] + cast_in,
        out_specs=[pl.BlockSpec((bm, bn), lambda m, n: (m, n))] + cast_out,
        out_shape=[jax.ShapeDtypeStruct((t, n), out_dtype)] + cast_shape,
        compiler_params=pltpu.CompilerParams(dimension_semantics=("parallel", "parallel"),
                                             vmem_limit_bytes=BIG_VMEM_LIMIT_BYTES),
        name=name,
    )(hb, ss, w, *[src for src, _ in cast_jobs])
    return outs[0], [w[None] for w in outs[1:]]


def _proj_rope(hb, ss, idx, w, layer, cos, sin, scale):
    _, t, _ = hb.shape
    n = w.shape[2]
    bm, bn = 1024, 1024
    return pl.pallas_call(
        functools.partial(_proj_rope_kernel, scale=scale),
        grid=(t // bm, n // bn),
        in_specs=_stream_specs(hb, idx, bm) + [
            _weight_spec(w, layer, bn),
            pl.BlockSpec((bm, LANES), lambda m, n: (m, 0)),
            pl.BlockSpec((bm, LANES), lambda m, n: (m, 0))],
        out_specs=pl.BlockSpec((bm, bn), lambda m, n: (m, n)),
        out_shape=jax.ShapeDtypeStruct((t, n), BF16),
        compiler_params=_cparams("parallel", "parallel"),
        name="q_proj_rope",
    )(hb, ss, w, cos, sin)


def _kv_proj(hb, ss, idx, w, cos, sin):
    _, t, _ = hb.shape
    n = w.shape[2]
    nk = n // 2
    bm = 1024
    return pl.pallas_call(
        _kv_kernel,
        grid=(t // bm, 1),
        in_specs=_stream_specs(hb, idx, bm) + [
            _weight_spec(w, 0, n),
            pl.BlockSpec((bm, LANES), lambda m, n: (m, 0)),
            pl.BlockSpec((bm, LANES), lambda m, n: (m, 0))],
        out_specs=[pl.BlockSpec((bm, nk), lambda m, n: (m, 0))] * 2,
        out_shape=[jax.ShapeDtypeStruct((t, nk), F32)] * 2,
        compiler_params=_cparams("parallel", "arbitrary"),
        name="kv_proj_rope",
    )(hb, ss, w, cos, sin)


def _resid_kernel(a_ref, w_ref, h_ref, g_ref, hn_ref, hb_ref, ss_ref):
    part = None
    for cs in _col_chunks(hn_ref.shape[1]):
        acc = jnp.dot(a_ref[...], w_ref[:, cs], preferred_element_type=F32)
        p = _emit_cols(h_ref[:, cs] + acc, cs, g_ref, hn_ref, hb_ref)
        part = p if part is None else part + p
    _accumulate_ss(ss_ref, part, pl.program_id(1) == 0)


def _ple_kernel(a_ref, ssin_ref, w_ref, p_ref, wup_ref, h_ref, g_ref, hn_ref, hb_ref, ss_ref):
    rstd = _rstd(ssin_ref, a_ref.shape[1])
    pb = p_ref[...].astype(BF16)
    part = None
    for cs in _col_chunks(hn_ref.shape[1]):
        z = jnp.dot(a_ref[...], w_ref[:, cs], preferred_element_type=F32) * rstd
        gate = 1.0 / (1.0 + jnp.exp(-z))
        up = jnp.dot(pb, wup_ref[:, cs], preferred_element_type=F32)
        p = _emit_cols(h_ref[:, cs] + gate * up, cs, g_ref, hn_ref, hb_ref)
        part = p if part is None else part + p
    _accumulate_ss(ss_ref, part, pl.program_id(1) == 0)


def _stream_out(t, d, ng, bm, bn):
    specs = [pl.BlockSpec((bm, bn), lambda m, n: (m, n)),
             pl.BlockSpec((ng, bm, bn), lambda m, n: (0, m, n)),
             pl.BlockSpec((bm, LANES), lambda m, n: (m, 0))]
    shapes = [jax.ShapeDtypeStruct((t, d), F32),
              jax.ShapeDtypeStruct((ng, t, d), BF16),
              jax.ShapeDtypeStruct((t, LANES), F32)]
    return specs, shapes


def _resid_matmul(a, w, layer, h, gains, name):
    t, k = a.shape
    d = w.shape[2]
    ng = gains.shape[0]
    bm, bn = 1024, 512
    out_specs, out_shape = _stream_out(t, d, ng, bm, bn)
    return pl.pallas_call(
        _resid_kernel,
        grid=(t // bm, d // bn),
        in_specs=[pl.BlockSpec((bm, k), lambda m, n: (m, 0)),
                  _weight_spec(w, layer, bn),
                  pl.BlockSpec((bm, bn), lambda m, n: (m, n)),
                  pl.BlockSpec((ng, bn), lambda m, n: (0, n))],
        out_specs=out_specs,
        out_shape=out_shape,
        compiler_params=_cparams("parallel", "arbitrary"),
        name=name,
    )(a, w, h, gains)


def _ple(hb, ss, w_gate, gate_layer, p, w_up, layer, h, gains):
    _, t, _ = hb.shape
    d = w_gate.shape[2]
    ng = gains.shape[0]
    pdim = p.shape[-1]
    bm, bn = 1024, 512
    out_specs, out_shape = _stream_out(t, d, ng, bm, bn)
    return pl.pallas_call(
        _ple_kernel,
        grid=(t // bm, d // bn),
        in_specs=_stream_specs(hb, 0, bm) + [
            _weight_spec(w_gate, gate_layer, bn),
            pl.BlockSpec((None, None, bm, pdim), lambda m, n: (layer, 0, m, 0)),
            _weight_spec(w_up, layer, bn),
            pl.BlockSpec((bm, bn), lambda m, n: (m, n)),
            pl.BlockSpec((ng, bn), lambda m, n: (0, n))],
        out_specs=out_specs,
        out_shape=out_shape,
        compiler_params=_cparams("parallel", "arbitrary"),
        name="ple",
    )(hb, ss, w_gate, p, w_up, h, gains)


def _mlp_kernel(*refs, n_ff, n_cast):
    a_ref, ssin_ref, w1_ref, w2_ref, h_ref, g_ref = refs[:6]
    cast_src = refs[6:6 + n_cast]
    hn_ref, hb_ref, ss_ref = refs[6 + n_cast:9 + n_cast]
    cast_dst = refs[9 + n_cast:9 + 2 * n_cast]
    acc_ref = refs[-1]
    f = pl.program_id(1)
    ec = acc_ref.shape[2]

    @pl.when(f == 0)
    def _():
        acc_ref[...] = jnp.zeros_like(acc_ref)

    @pl.when(f < n_ff)
    def _():
        z = jnp.dot(a_ref[...], w1_ref[...], preferred_element_type=F32)
        z = jnp.maximum(z * _rstd(ssin_ref, a_ref.shape[1]), 0.0)
        act = (z * z).astype(BF16)
        for j in range(acc_ref.shape[0]):
            acc_ref[j] += jnp.dot(act, w2_ref[:, j * ec:(j + 1) * ec], preferred_element_type=F32)
        _cast_blocks(cast_src, cast_dst)

    @pl.when(f >= n_ff)
    def _():
        part = _emit_cols(h_ref[...] + acc_ref[f - n_ff], slice(0, ec), g_ref, hn_ref, hb_ref)
        _accumulate_ss(ss_ref, part, f == n_ff)


def _mlp(hb, ss, w1, w2, layer, h, gains, cast_jobs=()):
    _, t, k = hb.shape
    d_ff = w1.shape[2]
    d = w2.shape[2]
    ng = gains.shape[0]
    bm, bf = 1024, 512
    ec = 512 if len(cast_jobs) <= 1 else 256
    n_ff, n_ec = d_ff // bf, d // ec
    last = n_ff - 1

    def epi(m, f):
        return (m, jnp.maximum(f - n_ff, 0))

    cast_in, cast_out, cast_shape = _cast_specs(cast_jobs, (t // bm) * n_ff,
                                                lambda m, f: m * n_ff + jnp.minimum(f, last))
    outs = pl.pallas_call(
        functools.partial(_mlp_kernel, n_ff=n_ff, n_cast=len(cast_jobs)),
        grid=(t // bm, n_ff + n_ec),
        in_specs=[
            pl.BlockSpec((None, bm, k), lambda m, f: (0, m, 0), pipeline_mode=pl.Buffered(1)),
            pl.BlockSpec((bm, LANES), lambda m, f: (m, 0)),
            pl.BlockSpec((None, k, bf), lambda m, f: (layer, 0, jnp.minimum(f, last))),
            pl.BlockSpec((None, bf, d), lambda m, f: (layer, jnp.minimum(f, last), 0)),
            pl.BlockSpec((bm, ec), epi),
            pl.BlockSpec((ng, ec), lambda m, f: (0, jnp.maximum(f - n_ff, 0)))] + cast_in,
        out_specs=[pl.BlockSpec((bm, ec), epi),
                   pl.BlockSpec((ng, bm, ec), lambda m, f: (0, m, jnp.maximum(f - n_ff, 0))),
                   pl.BlockSpec((bm, LANES), lambda m, f: (m, 0))] + cast_out,
        out_shape=[jax.ShapeDtypeStruct((t, d), F32),
                   jax.ShapeDtypeStruct((ng, t, d), BF16),
                   jax.ShapeDtypeStruct((t, LANES), F32)] + cast_shape,
        scratch_shapes=[pltpu.VMEM((n_ec, bm, ec), F32)],
        compiler_params=pltpu.CompilerParams(dimension_semantics=("parallel", "arbitrary"),
                                             vmem_limit_bytes=BIG_VMEM_LIMIT_BYTES),
        name="mlp",
    )(hb, ss, w1, w2, h, gains, *[src for src, _ in cast_jobs])
    return outs[:3], [w[None] for w in outs[3:]]


def _hgrn_kernel(*refs, layer, n_cast):
    q_ref, f_ref, i_ref, g_ref, lbl_ref, ong_ref = refs[:6]
    o_ref = refs[6 + n_cast]
    (st_ref, b_ref, k_ref, qin_ref, qe_ref, ke_ref, kd_ref, dl_ref, gate_ref, oacc_ref,
     upd_ref) = refs[7 + 2 * n_cast:]
    _cast_blocks(refs[6:6 + n_cast], refs[7 + n_cast:7 + 2 * n_cast])
    _hgrn_body(q_ref, f_ref, i_ref, g_ref, lbl_ref, ong_ref, o_ref, st_ref, b_ref, k_ref, qin_ref,
               qe_ref, ke_ref, kd_ref, dl_ref, gate_ref, oacc_ref, upd_ref, layer)


def _hgrn_body(q_ref, f_ref, i_ref, g_ref, lbl_ref, ong_ref, o_ref,
               st_ref, b_ref, k_ref, qin_ref, qe_ref, ke_ref, kd_ref, dl_ref, gate_ref, oacc_ref, upd_ref,
               layer):
    bt = q_ref.shape[0]
    c_len = HG_CHUNK
    half = c_len // 2
    n_chunks = bt // c_len

    @pl.when(pl.program_id(1) == 0)
    def _():
        st_ref[...] = jnp.zeros_like(st_ref)

    lg = lbl_ref[...]
    e = jnp.exp(lg - jnp.max(lg, axis=0, keepdims=True))
    sm = e / jnp.sum(e, axis=0, keepdims=True)
    cs = sm[0:1]
    for l in range(1, layer + 1):
        cs = cs + sm[l:l + 1]
    lb = cs - sm[0:1]

    sig = 1.0 / (1.0 + jnp.exp(-f_ref[...]))
    fgate = lb + (1.0 - lb) * sig
    kk = 1.0 - fgate
    k_ref[...] = kk
    logf = jnp.log(fgate)
    row = lax.broadcasted_iota(jnp.int32, (c_len, HG_HEAD_DIM), 0)
    shifts = [1 << s for s in range(c_len.bit_length() - 1)]
    keep = [row >= shift for shift in shifts]

    for c in range(n_chunks):
        rows = slice(c * c_len, (c + 1) * c_len)
        bc, kc, q = logf[rows], kk[rows], q_ref[rows, :]
        for shift, mask in zip(shifts, keep):
            bc = bc + jnp.where(mask, pltpu.roll(bc, shift, axis=0), 0.0)
        b_ref[rows, :] = bc
        bl = bc[c_len - 1:c_len, :]
        w = bc - bc[half - 1:half, :]
        qin_ref[rows, :] = (q * jnp.exp(bc)).astype(BF16)
        qe_ref[rows, :] = (q * jnp.exp(w)).astype(BF16)
        ke_ref[rows, :] = (kc * jnp.exp(-w)).astype(BF16)
        kd_ref[rows, :] = (kc * jnp.exp(bl - bc)).astype(BF16)
        dl_ref[c:c + 1, :] = jnp.exp(bl)
    gg = g_ref[...]
    gate_ref[...] = ong_ref[...] * gg / (1.0 + jnp.exp(-gg))

    mids = b_ref[pl.ds(half - 1, bt // half, stride=half), :]
    second = (lax.broadcasted_iota(jnp.int32, mids.shape, 0) & 1) == 1
    safe = jnp.max(jnp.where(second, pltpu.roll(mids, 1, axis=0), 0.0) - mids) <= HG_SAFE_DECAY

    tri = (lax.broadcasted_iota(jnp.int32, (c_len, c_len), 1)
           <= lax.broadcasted_iota(jnp.int32, (c_len, c_len), 0))
    rowc = lax.broadcasted_iota(jnp.int32, (c_len, 1), 0)

    def state_increment(sl):
        return jnp.dot(i_ref[sl, :].T.astype(BF16), kd_ref[sl, :], preferred_element_type=F32)

    def finish():
        o = oacc_ref[...]
        on = o * lax.rsqrt(jnp.mean(o * o, axis=-1, keepdims=True) + NORM_EPS)
        o_ref[...] = (on * gate_ref[...]).astype(BF16)

    @pl.when(safe)
    def _():
        sls = [slice(c * c_len, (c + 1) * c_len) for c in range(n_chunks)]
        atts = [lax.dot_general(qe_ref[sl, :], ke_ref[sl, :], DN_LAST, preferred_element_type=F32)
                for sl in sls]
        atts = [jnp.where(tri, att, 0.0).astype(BF16) for att in atts]
        for sl, att in zip(sls, atts):
            oacc_ref[sl, :] = jnp.dot(att, i_ref[sl, :].astype(BF16), preferred_element_type=F32)
        for c, sl in enumerate(sls):
            upd_ref[c] = state_increment(sl)
        st = st_ref[...]
        for c, sl in enumerate(sls):
            oacc_ref[sl, :] += lax.dot_general(qin_ref[sl, :], st.astype(BF16), DN_LAST,
                                               preferred_element_type=F32)
            st = st * dl_ref[c:c + 1, :] + upd_ref[c]
        st_ref[...] = st
        finish()

    @pl.when(jnp.logical_not(safe))
    def _():
        def body(c, carry):
            r0 = pl.multiple_of(c * c_len, c_len)
            sl = pl.ds(r0, c_len)
            st = st_ref[...]
            q, bc = q_ref[sl, :], b_ref[sl, :]

            def key_row(s, acc):
                bs = b_ref[pl.ds(r0 + s, 1), :]
                ks = k_ref[pl.ds(r0 + s, 1), :]
                vs = i_ref[pl.ds(r0 + s, 1), :]
                a = jnp.sum(q * ks * jnp.exp(jnp.minimum(bc - bs, 0.0)), axis=-1, keepdims=True)
                return acc + jnp.where(rowc >= s, a, 0.0) * vs

            o = lax.dot_general(qin_ref[sl, :], st.astype(BF16), DN_LAST, preferred_element_type=F32)
            oacc_ref[sl, :] = o + lax.fori_loop(0, c_len, key_row, jnp.zeros((c_len, HG_HEAD_DIM), F32))
            st_ref[...] = st * dl_ref[pl.ds(c, 1), :] + state_increment(sl)
            return carry
        lax.fori_loop(0, n_chunks, body, 0)
        finish()


def _hgrn_scan(proj, lb_logits, out_norm_g, layer, cast_jobs=()):
    t = proj.shape[0]
    d = proj.shape[1] // 4
    heads = d // HG_HEAD_DIM
    n_a = lb_logits.shape[0]
    bt = 1024

    def col(part):
        return pl.BlockSpec((bt, HG_HEAD_DIM), lambda h, i: (i, part * heads + h))

    def rows(dtype):
        return pltpu.VMEM((bt, HG_HEAD_DIM), dtype)

    n_row = t // bt
    cast_in, cast_out, cast_shape = _cast_specs(cast_jobs, heads * n_row, lambda h, i: h * n_row + i)
    outs = pl.pallas_call(
        functools.partial(_hgrn_kernel, layer=layer, n_cast=len(cast_jobs)),
        grid=(heads, n_row),
        in_specs=[col(0), col(1), col(2), col(3),
                  pl.BlockSpec((n_a, HG_HEAD_DIM), lambda h, i: (0, h)),
                  pl.BlockSpec((None, 1, HG_HEAD_DIM), lambda h, i: (layer, 0, h))] + cast_in,
        out_specs=[pl.BlockSpec((bt, HG_HEAD_DIM), lambda h, i: (i, h))] + cast_out,
        out_shape=[jax.ShapeDtypeStruct((t, d), BF16)] + cast_shape,
        scratch_shapes=[pltpu.VMEM((HG_HEAD_DIM, HG_HEAD_DIM), F32),
                        rows(F32), rows(F32),
                        rows(BF16), rows(BF16), rows(BF16), rows(BF16),
                        pltpu.VMEM((bt // HG_CHUNK, HG_HEAD_DIM), F32),
                        rows(F32), rows(F32),
                        pltpu.VMEM((bt // HG_CHUNK, HG_HEAD_DIM, HG_HEAD_DIM), F32)],
        compiler_params=_cparams("parallel", "arbitrary"),
        name="hgrn_scan",
    )(proj, proj, proj, proj, lb_logits, out_norm_g.reshape(n_a, 1, d), *[src for src, _ in cast_jobs])
    return outs[0], [w[None] for w in outs[1:]]


def _attn_kernel(sink_ref, bias_ref, q_ref, kc_ref, kp_ref, vc_ref, vp_ref, o_ref):
    jj = pl.program_id(1)
    heads_per_kv = q_ref.shape[1] // (2 * ATT_HEAD_DIM)
    n_slab = heads_per_kv // 2

    k2 = jnp.concatenate([kp_ref[...], kc_ref[...]], axis=0)
    v2 = jnp.concatenate([vp_ref[...], vc_ref[...]], axis=0)
    k2r, v2r = pltpu.roll(k2, ATT_HEAD_DIM, axis=1), pltpu.roll(v2, ATT_HEAD_DIM, axis=1)
    low = lax.broadcasted_iota(jnp.int32, k2.shape, 1) < ATT_HEAD_DIM
    ones_lo = jnp.where(low, 1.0, 0.0)
    real_key = lax.broadcasted_iota(jnp.int32, k2.shape, 0) > 0
    v_low, v_high = low & real_key, jnp.logical_not(low) & real_key
    sink_col = lax.broadcasted_iota(jnp.int32, (WINDOW, LANES), 1) == 0
    bias = bias_ref[...]

    scores, v_exts = [], []
    for kvh in range(2):
        k_lo, k_hi = (k2, k2r) if kvh == 0 else (k2r, k2)
        v_lo, v_hi = (v2, v2r) if kvh == 0 else (v2r, v2)
        k_half = (jnp.where(low, k_lo, 0.0).astype(BF16), jnp.where(low, 0.0, k_hi).astype(BF16))
        v_exts.append((jnp.concatenate([jnp.where(v_low, v_lo, 0.0), ones_lo], axis=1).astype(BF16),
                       jnp.concatenate([jnp.where(v_high, v_hi, 0.0), 1.0 - ones_lo], axis=1).astype(BF16)))
        col0 = kvh * n_slab * LANES
        qs = jnp.concatenate([q_ref[:, col0 + p * LANES:col0 + (p + 1) * LANES] for p in range(n_slab)],
                             axis=0)
        scores.append([lax.dot_general(qs, k_half[parity], DN_LAST, preferred_element_type=F32)
                       for parity in range(2)])

    probs = []
    for kvh in range(2):
        pe = []
        for parity in range(2):
            head0 = (2 * jj + kvh) * heads_per_kv + parity
            slabs = []
            for p in range(n_slab):
                sp = scores[kvh][parity][p * WINDOW:(p + 1) * WINDOW, :] + bias
                sp = jnp.concatenate([jnp.where(sink_col, sink_ref[head0 + 2 * p], sp[:, :LANES]),
                                      sp[:, LANES:]], axis=1)
                slabs.append(jnp.exp(sp - jnp.max(sp, axis=-1, keepdims=True)).astype(BF16))
            pe.append(jnp.concatenate(slabs, axis=0))
        probs.append(pe)

    for kvh in range(2):
        res = [jnp.dot(probs[kvh][parity], v_exts[kvh][parity], preferred_element_type=F32)
               for parity in range(2)]
        out = (res[0][:, :LANES] + res[1][:, :LANES]) / (res[0][:, LANES:] + res[1][:, LANES:])
        col0 = kvh * n_slab * LANES
        for p in range(n_slab):
            o_ref[:, col0 + p * LANES:col0 + (p + 1) * LANES] = out[p * WINDOW:(p + 1) * WINDOW, :].astype(BF16)


def _attention(q, k, v, sinks):
    t, dq = q.shape
    nb = t // WINDOW
    gw = 2 * dq // ATT_KV_HEADS
    a_idx = jnp.arange(WINDOW)[:, None]
    c_idx = jnp.arange(2 * WINDOW)[None, :]
    band = (c_idx >= a_idx + 1) & (c_idx <= a_idx + WINDOW)
    bias = jnp.stack([jnp.where(band & (c_idx >= WINDOW), 0.0, -jnp.inf),
                      jnp.where(band, 0.0, -jnp.inf)]).astype(F32)

    def cur(n, j):
        return (n, j)

    def prev(n, j):
        return (jnp.maximum(n - 1, 0), j)

    return pl.pallas_call(
        _attn_kernel,
        grid=(nb, ATT_KV_HEADS // 2),
        in_specs=[pl.BlockSpec(memory_space=pltpu.SMEM),
                  pl.BlockSpec((None, WINDOW, 2 * WINDOW), lambda n, j: (jnp.minimum(n, 1), 0, 0)),
                  pl.BlockSpec((WINDOW, gw), lambda n, j: (n, j)),
                  pl.BlockSpec((WINDOW, LANES), cur),
                  pl.BlockSpec((WINDOW, LANES), prev),
                  pl.BlockSpec((WINDOW, LANES), cur),
                  pl.BlockSpec((WINDOW, LANES), prev)],
        out_specs=pl.BlockSpec((WINDOW, gw), lambda n, j: (n, j)),
        out_shape=jax.ShapeDtypeStruct((t, dq), BF16),
        compiler_params=_cparams("parallel", "parallel"),
        name="swa_attention",
    )(sinks, bias, q, k, k, v, v)


def kernel(x, p, positions, mixer_norm_g, hgrn_w_in, hgrn_w_out, hgrn_lb_logits, hgrn_out_norm_g,
           kv_norm_g, w_kv, attn_w_q, attn_w_o, attn_sinks, mlp_norm_g, mlp_w1, mlp_w2,
           ple_norm_g, ple_w_gate, ple_w_up, final_norm_g):
    batch, t, d = x.shape
    depth = mlp_w1.shape[0]
    n_a = hgrn_w_in.shape[0]
    assert batch == 1

    w_in, w_kvb, w_up = hgrn_w_in[0:1].astype(BF16), w_kv.astype(BF16)[None], ple_w_up.astype(BF16)
    mixer_w = [hgrn_w_out[0:1].astype(BF16)] if n_a > 0 else [attn_w_q[0:1].astype(BF16),
                                                              attn_w_o[0:1].astype(BF16)]
    if n_a == 0:
        w1, w2 = mlp_w1[0:1].astype(BF16), mlp_w2[0:1].astype(BF16)

    cos, sin = _rope_tables(positions)
    h = x.reshape(t, d)
    hb, ss = _prep_stream(h, mixer_norm_g[0])
    k_sh = v_sh = None

    for layer in range(depth):
        mlp_gain = mlp_norm_g[layer][None, :]
        if layer < n_a:
            jobs = [(mlp_w1, 0), (mlp_w2, 0)] if layer == 0 else []
            proj, cast = _proj(hb, ss, 0, w_in, 0, F32, "hgrn_in_proj", jobs)
            if layer == 0:
                w1, w2 = cast
            jobs = [(hgrn_w_in, layer + 1)] if layer + 1 < n_a else []
            mixed, cast = _hgrn_scan(proj, hgrn_lb_logits, hgrn_out_norm_g, layer, jobs)
            if jobs:
                w_in = cast[0]
            h, hb, ss = _resid_matmul(mixed, mixer_w[0], 0, h, mlp_gain, "hgrn_out_proj")
        else:
            j = layer - n_a
            q = _proj_rope(hb, ss, 0, mixer_w[0], 0, cos, sin, ATT_HEAD_DIM ** -0.5)
            mixed = _attention(q, k_sh, v_sh, attn_sinks[j])
            h, hb, ss = _resid_matmul(mixed, mixer_w[1], 0, h, mlp_gain, "attn_out_proj")

        jobs = [(ple_w_gate, layer)]
        if layer + 1 < depth:
            jobs += [(mlp_w1, layer + 1), (mlp_w2, layer + 1)]
            jobs += ([(hgrn_w_out, layer + 1)] if layer + 1 < n_a
                     else [(attn_w_q, layer + 1 - n_a), (attn_w_o, layer + 1 - n_a)])
        (h, hb, ss), cast = _mlp(hb, ss, w1, w2, 0, h, ple_norm_g[layer][None, :], jobs)
        w_gate = cast[0]
        if layer + 1 < depth:
            w1, w2, mixer_w = cast[1], cast[2], cast[3:]

        if layer + 1 < depth:
            next_gains = [mixer_norm_g[layer + 1]]
            if layer == n_a - 1:
                next_gains.append(kv_norm_g)
        else:
            next_gains = [final_norm_g]
        h, hb, ss = _ple(hb, ss, w_gate, 0, p, w_up, layer, h, jnp.stack(next_gains))

        if layer == n_a - 1:
            k_sh, v_sh = _kv_proj(hb, ss, 1, w_kvb, cos, sin)

    return _final_norm(h, ss, final_norm_g).reshape(batch, t, d)
```

```python
import functools

import jax
import jax.numpy as jnp
from jax import lax
from jax.experimental import pallas as pl
from jax.experimental.pallas import tpu as pltpu

F32 = jnp.float32
BF16 = jnp.bfloat16

LANES = 128
MXU_COLS = 256
VMEM_LIMIT_BYTES = 56 * 1024 * 1024
BIG_VMEM_LIMIT_BYTES = 63 * 1024 * 1024

HG_HEAD_DIM = 128
HG_CHUNK = 64
HG_SAFE_DECAY = 60.0
ATT_HEAD_DIM = 64
ATT_KV_HEADS = 8
WINDOW = 128
ROPE_THETA = 10000.0
NORM_EPS = 1e-6

DN_LAST = (((1,), (1,)), ((), ()))


def _cparams(*sem):
    return pltpu.CompilerParams(dimension_semantics=sem, vmem_limit_bytes=VMEM_LIMIT_BYTES)


def _rstd(ss_ref, d_model):
    return lax.rsqrt(jnp.sum(ss_ref[...], axis=-1, keepdims=True) * (1.0 / d_model) + NORM_EPS)


def _lane_partial_sumsq(x):
    sq = x * x
    acc = sq[:, 0:LANES]
    for c in range(1, x.shape[1] // LANES):
        acc = acc + sq[:, c * LANES:(c + 1) * LANES]
    return acc


def _col_chunks(width):
    return [slice(c, c + MXU_COLS) for c in range(0, width, MXU_COLS)]


def _emit_cols(h_new, cs, g_ref, hn_ref, hb_ref):
    hn_ref[:, cs] = h_new
    for i in range(hb_ref.shape[0]):
        hb_ref[i, :, cs] = (h_new * g_ref[i:i + 1, cs]).astype(BF16)
    return _lane_partial_sumsq(h_new)


def _accumulate_ss(ss_ref, part, first):
    @pl.when(first)
    def _():
        ss_ref[...] = jnp.zeros_like(ss_ref)
    ss_ref[...] += part


def _cast_specs(jobs, n_steps, step_of):
    in_specs, out_specs, out_shape = [], [], []
    for src, layer in jobs:
        _, k, n = src.shape
        rows = k // n_steps
        assert rows * n_steps == k and rows % 16 == 0
        in_specs.append(pl.BlockSpec((None, rows, n), lambda *g, layer=layer: (layer, step_of(*g), 0)))
        out_specs.append(pl.BlockSpec((rows, n), lambda *g: (step_of(*g), 0)))
        out_shape.append(jax.ShapeDtypeStruct((k, n), BF16))
    return in_specs, out_specs, out_shape


def _cast_blocks(src_refs, dst_refs):
    for src, dst in zip(src_refs, dst_refs):
        dst[...] = src[...].astype(BF16)


def _swap_half_heads(x):
    lane = lax.broadcasted_iota(jnp.int32, x.shape, 1)
    first_half = (lane & (ATT_HEAD_DIM - 1)) < (ATT_HEAD_DIM // 2)
    return jnp.where(first_half, pltpu.roll(x, LANES - ATT_HEAD_DIM // 2, axis=1),
                     pltpu.roll(x, ATT_HEAD_DIM // 2, axis=1))


def _rope_store(x, cos, sin_signed, o_ref, col0):
    for c in range(x.shape[1] // LANES):
        xc = x[:, c * LANES:(c + 1) * LANES]
        piece = xc * cos + _swap_half_heads(xc) * sin_signed
        o_ref[:, col0 + c * LANES:col0 + (c + 1) * LANES] = piece.astype(o_ref.dtype)


def _rope_table_kernel(pos_ref, invf_ref, sign_ref, cos_ref, sin_ref):
    ang = pos_ref[...].astype(F32) * invf_ref[...]
    cos_ref[...] = jnp.cos(ang)
    sin_ref[...] = jnp.sin(ang) * sign_ref[...]


def _rope_tables(positions):
    t = positions.shape[-1]
    half = ATT_HEAD_DIM // 2
    inv_freq = ROPE_THETA ** (-jnp.arange(half, dtype=F32) / half)
    invf = jnp.tile(inv_freq, LANES // half)[None, :]
    sign = jnp.tile(jnp.concatenate([-jnp.ones((half,), F32), jnp.ones((half,), F32)]),
                    LANES // ATT_HEAD_DIM)[None, :]
    bt = 1024
    return pl.pallas_call(
        _rope_table_kernel,
        grid=(t // bt,),
        in_specs=[pl.BlockSpec((bt, 1), lambda i: (i, 0)),
                  pl.BlockSpec((1, LANES), lambda i: (0, 0)),
                  pl.BlockSpec((1, LANES), lambda i: (0, 0))],
        out_specs=[pl.BlockSpec((bt, LANES), lambda i: (i, 0))] * 2,
        out_shape=[jax.ShapeDtypeStruct((t, LANES), F32)] * 2,
        compiler_params=_cparams("parallel"),
        name="rope_tables",
    )(positions.reshape(t, 1), invf, sign)


def _prep_kernel(x_ref, g_ref, hb_ref, ss_ref):
    x = x_ref[...]
    hb_ref[0] = (x * g_ref[...]).astype(BF16)
    ss_ref[...] = _lane_partial_sumsq(x)


def _prep_stream(x, g):
    t, d = x.shape
    bt = 256
    return pl.pallas_call(
        _prep_kernel,
        grid=(t // bt,),
        in_specs=[pl.BlockSpec((bt, d), lambda i: (i, 0)),
                  pl.BlockSpec((1, d), lambda i: (0, 0))],
        out_specs=[pl.BlockSpec((1, bt, d), lambda i: (0, i, 0)),
                   pl.BlockSpec((bt, LANES), lambda i: (i, 0))],
        out_shape=[jax.ShapeDtypeStruct((1, t, d), BF16),
                   jax.ShapeDtypeStruct((t, LANES), F32)],
        compiler_params=_cparams("parallel"),
        name="prep_stream",
    )(x, g.reshape(1, d))


def _final_norm_kernel(h_ref, ss_ref, g_ref, o_ref):
    o_ref[...] = h_ref[...] * _rstd(ss_ref, h_ref.shape[1]) * g_ref[...]


def _final_norm(h, ss, g):
    t, d = h.shape
    bt = 256
    return pl.pallas_call(
        _final_norm_kernel,
        grid=(t // bt,),
        in_specs=[pl.BlockSpec((bt, d), lambda i: (i, 0)),
                  pl.BlockSpec((bt, LANES), lambda i: (i, 0)),
                  pl.BlockSpec((1, d), lambda i: (0, 0))],
        out_specs=pl.BlockSpec((bt, d), lambda i: (i, 0)),
        out_shape=jax.ShapeDtypeStruct((t, d), F32),
        compiler_params=_cparams("parallel"),
        name="final_norm",
    )(h, ss, g.reshape(1, d))


def _proj_kernel(*refs, n_cast):
    a_ref, ss_ref, w_ref = refs[:3]
    o_ref = refs[3 + n_cast]
    acc = jnp.dot(a_ref[...], w_ref[...], preferred_element_type=F32)
    o_ref[...] = (acc * _rstd(ss_ref, a_ref.shape[1])).astype(o_ref.dtype)
    _cast_blocks(refs[3:3 + n_cast], refs[4 + n_cast:])


def _proj_rope_kernel(a_ref, ss_ref, w_ref, cos_ref, sin_ref, o_ref, *, scale):
    row_scale = _rstd(ss_ref, a_ref.shape[1]) * scale
    cos, sin = cos_ref[...], sin_ref[...]
    for cs in _col_chunks(o_ref.shape[1]):
        acc = jnp.dot(a_ref[...], w_ref[:, cs], preferred_element_type=F32)
        _rope_store(acc * row_scale, cos, sin, o_ref, cs.start)


def _kv_kernel(a_ref, ss_ref, w_ref, cos_ref, sin_ref, k_ref, v_ref):
    rstd = _rstd(ss_ref, a_ref.shape[1])
    nk = k_ref.shape[1]
    cos, sin = cos_ref[...], sin_ref[...]
    for cs in _col_chunks(nk):
        acc = jnp.dot(a_ref[...], w_ref[:, cs], preferred_element_type=F32)
        _rope_store(acc * rstd, cos, sin, k_ref, cs.start)
    for cs in _col_chunks(nk):
        acc = jnp.dot(a_ref[...], w_ref[:, nk + cs.start:nk + cs.stop], preferred_element_type=F32)
        v_ref[:, cs] = acc * rstd


def _stream_specs(hb, idx, bm, single_buffer=False):
    k = hb.shape[2]
    mode = dict(pipeline_mode=pl.Buffered(1)) if single_buffer else {}
    return [pl.BlockSpec((None, bm, k), lambda m, n: (idx, m, 0), **mode),
            pl.BlockSpec((bm, LANES), lambda m, n: (m, 0))]


def _weight_spec(w, layer, bn):
    return pl.BlockSpec((None, w.shape[1], bn), lambda m, n: (layer, 0, n))


def _proj(hb, ss, idx, w, layer, out_dtype, name, cast_jobs=()):
    _, t, _ = hb.shape
    n = w.shape[2]
    bm, bn = 1024, 1024
    n_col = n // bn
    cast_in, cast_out, cast_shape = _cast_specs(cast_jobs, (t // bm) * n_col, lambda m, n: m * n_col + n)
    outs = pl.pallas_call(
        functools.partial(_proj_kernel, n_cast=len(cast_jobs)),
        grid=(t // bm, n_col),
        in_specs=_stream_specs(hb, idx, bm, bool(cast_jobs)) + [_weight_spec(w, layer, bn)] + cast_in,
        out_specs=[pl.BlockSpec((bm, bn), lambda m, n: (m, n))] + cast_out,
        out_shape=[jax.ShapeDtypeStruct((t, n), out_dtype)] + cast_shape,
        compiler_params=pltpu.CompilerParams(dimension_semantics=("parallel", "parallel"),
                                             vmem_limit_bytes=BIG_VMEM_LIMIT_BYTES),
        name=name,
    )(hb, ss, w, *[src for src, _ in cast_jobs])
    return outs[0], [w[None] for w in outs[1:]]


def _proj_rope(hb, ss, idx, w, layer, cos, sin, scale):
    _, t, _ = hb.shape
    n = w.shape[2]
    bm, bn = 1024, 1024
    return pl.pallas_call(
        functools.partial(_proj_rope_kernel, scale=scale),
        grid=(t // bm, n // bn),
        in_specs=_stream_specs(hb, idx, bm) + [
            _weight_spec(w, layer, bn),
            pl.BlockSpec((bm, LANES), lambda m, n: (m, 0)),
            pl.BlockSpec((bm, LANES), lambda m, n: (m, 0))],
        out_specs=pl.BlockSpec((bm, bn), lambda m, n: (m, n)),
        out_shape=jax.ShapeDtypeStruct((t, n), BF16),
        compiler_params=_cparams("parallel", "parallel"),
        name="q_proj_rope",
    )(hb, ss, w, cos, sin)


def _kv_proj(hb, ss, idx, w, cos, sin):
    _, t, _ = hb.shape
    n = w.shape[2]
    nk = n // 2
    bm = 1024
    return pl.pallas_call(
        _kv_kernel,
        grid=(t // bm, 1),
        in_specs=_stream_specs(hb, idx, bm) + [
            _weight_spec(w, 0, n),
            pl.BlockSpec((bm, LANES), lambda m, n: (m, 0)),
            pl.BlockSpec((bm, LANES), lambda m, n: (m, 0))],
        out_specs=[pl.BlockSpec((bm, nk), lambda m, n: (m, 0))] * 2,
        out_shape=[jax.ShapeDtypeStruct((t, nk), F32)] * 2,
        compiler_params=_cparams("parallel", "arbitrary"),
        name="kv_proj_rope",
    )(hb, ss, w, cos, sin)


def _resid_kernel(a_ref, w_ref, h_ref, g_ref, hn_ref, hb_ref, ss_ref):
    part = None
    for cs in _col_chunks(hn_ref.shape[1]):
        acc = jnp.dot(a_ref[...], w_ref[:, cs], preferred_element_type=F32)
        p = _emit_cols(h_ref[:, cs] + acc, cs, g_ref, hn_ref, hb_ref)
        part = p if part is None else part + p
    _accumulate_ss(ss_ref, part, pl.program_id(1) == 0)


def _ple_kernel(a_ref, ssin_ref, w_ref, p_ref, wup_ref, h_ref, g_ref, hn_ref, hb_ref, ss_ref):
    rstd = _rstd(ssin_ref, a_ref.shape[1])
    pb = p_ref[...].astype(BF16)
    part = None
    for cs in _col_chunks(hn_ref.shape[1]):
        z = jnp.dot(a_ref[...], w_ref[:, cs], preferred_element_type=F32) * rstd
        gate = 1.0 / (1.0 + jnp.exp(-z))
        up = jnp.dot(pb, wup_ref[:, cs], preferred_element_type=F32)
        p = _emit_cols(h_ref[:, cs] + gate * up, cs, g_ref, hn_ref, hb_ref)
        part = p if part is None else part + p
    _accumulate_ss(ss_ref, part, pl.program_id(1) == 0)


def _stream_out(t, d, ng, bm, bn):
    specs = [pl.BlockSpec((bm, bn), lambda m, n: (m, n)),
             pl.BlockSpec((ng, bm, bn), lambda m, n: (0, m, n)),
             pl.BlockSpec((bm, LANES), lambda m, n: (m, 0))]
    shapes = [jax.ShapeDtypeStruct((t, d), F32),
              jax.ShapeDtypeStruct((ng, t, d), BF16),
              jax.ShapeDtypeStruct((t, LANES), F32)]
    return specs, shapes


def _resid_matmul(a, w, layer, h, gains, name):
    t, k = a.shape
    d = w.shape[2]
    ng = gains.shape[0]
    bm, bn = 1024, 512
    out_specs, out_shape = _stream_out(t, d, ng, bm, bn)
    return pl.pallas_call(
        _resid_kernel,
        grid=(t // bm, d // bn),
        in_specs=[pl.BlockSpec((bm, k), lambda m, n: (m, 0)),
                  _weight_spec(w, layer, bn),
                  pl.BlockSpec((bm, bn), lambda m, n: (m, n)),
                  pl.BlockSpec((ng, bn), lambda m, n: (0, n))],
        out_specs=out_specs,
        out_shape=out_shape,
        compiler_params=_cparams("parallel", "arbitrary"),
        name=name,
    )(a, w, h, gains)


def _ple(hb, ss, w_gate, gate_layer, p, w_up, layer, h, gains):
    _, t, _ = hb.shape
    d = w_gate.shape[2]
    ng = gains.shape[0]
    pdim = p.shape[-1]
    bm, bn = 1024, 512
    out_specs, out_shape = _stream_out(t, d, ng, bm, bn)
    return pl.pallas_call(
        _ple_kernel,
        grid=(t // bm, d // bn),
        in_specs=_stream_specs(hb, 0, bm) + [
            _weight_spec(w_gate, gate_layer, bn),
            pl.BlockSpec((None, None, bm, pdim), lambda m, n: (layer, 0, m, 0)),
            _weight_spec(w_up, layer, bn),
            pl.BlockSpec((bm, bn), lambda m, n: (m, n)),
            pl.BlockSpec((ng, bn), lambda m, n: (0, n))],
        out_specs=out_specs,
        out_shape=out_shape,
        compiler_params=_cparams("parallel", "arbitrary"),
        name="ple",
    )(hb, ss, w_gate, p, w_up, h, gains)


def _mlp_kernel(*refs, n_ff, n_cast):
    a_ref, ssin_ref, w1_ref, w2_ref, h_ref, g_ref = refs[:6]
    cast_src = refs[6:6 + n_cast]
    hn_ref, hb_ref, ss_ref = refs[6 + n_cast:9 + n_cast]
    cast_dst = refs[9 + n_cast:9 + 2 * n_cast]
    acc_ref = refs[-1]
    f = pl.program_id(1)
    ec = acc_ref.shape[2]

    @pl.when(f == 0)
    def _():
        acc_ref[...] = jnp.zeros_like(acc_ref)

    @pl.when(f < n_ff)
    def _():
        z = jnp.dot(a_ref[...], w1_ref[...], preferred_element_type=F32)
        z = jnp.maximum(z * _rstd(ssin_ref, a_ref.shape[1]), 0.0)
        act = (z * z).astype(BF16)
        for j in range(acc_ref.shape[0]):
            acc_ref[j] += jnp.dot(act, w2_ref[:, j * ec:(j + 1) * ec], preferred_element_type=F32)
        _cast_blocks(cast_src, cast_dst)

    @pl.when(f >= n_ff)
    def _():
        part = _emit_cols(h_ref[...] + acc_ref[f - n_ff], slice(0, ec), g_ref, hn_ref, hb_ref)
        _accumulate_ss(ss_ref, part, f == n_ff)


def _mlp(hb, ss, w1, w2, layer, h, gains, cast_jobs=()):
    _, t, k = hb.shape
    d_ff = w1.shape[2]
    d = w2.shape[2]
    ng = gains.shape[0]
    bm, bf = 1024, 512
    ec = 512
    n_ff, n_ec = d_ff // bf, d // ec
    last = n_ff - 1

    def epi(m, f):
        return (m, jnp.maximum(f - n_ff, 0))

    cast_in, cast_out, cast_shape = _cast_specs(cast_jobs, (t // bm) * n_ff,
                                                lambda m, f: m * n_ff + jnp.minimum(f, last))
    outs = pl.pallas_call(
        functools.partial(_mlp_kernel, n_ff=n_ff, n_cast=len(cast_jobs)),
        grid=(t // bm, n_ff + n_ec),
        in_specs=[
            pl.BlockSpec((None, bm, k), lambda m, f: (0, m, 0), pipeline_mode=pl.Buffered(1)),
            pl.BlockSpec((bm, LANES), lambda m, f: (m, 0)),
            pl.BlockSpec((None, k, bf), lambda m, f: (layer, 0, jnp.minimum(f, last))),
            pl.BlockSpec((None, bf, d), lambda m, f: (layer, jnp.minimum(f, last), 0)),
            pl.BlockSpec((bm, ec), epi),
            pl.BlockSpec((ng, ec), lambda m, f: (0, jnp.maximum(f - n_ff, 0)))] + cast_in,
        out_specs=[pl.BlockSpec((bm, ec), epi),
                   pl.BlockSpec((ng, bm, ec), lambda m, f: (0, m, jnp.maximum(f - n_ff, 0))),
                   pl.BlockSpec((bm, LANES), lambda m, f: (m, 0))] + cast_out,
        out_shape=[jax.ShapeDtypeStruct((t, d), F32),
                   jax.ShapeDtypeStruct((ng, t, d), BF16),
                   jax.ShapeDtypeStruct((t, LANES), F32)] + cast_shape,
        scratch_shapes=[pltpu.VMEM((n_ec, bm, ec), F32)],
        compiler_params=pltpu.CompilerParams(dimension_semantics=("parallel", "arbitrary"),
                                             vmem_limit_bytes=BIG_VMEM_LIMIT_BYTES),
        name="mlp",
    )(hb, ss, w1, w2, h, gains, *[src for src, _ in cast_jobs])
    return outs[:3], [w[None] for w in outs[3:]]


def _hgrn_kernel(*refs, layer, n_cast):
    q_ref, f_ref, i_ref, g_ref, lbl_ref, ong_ref = refs[:6]
    o_ref = refs[6 + n_cast]
    (st_ref, b_ref, k_ref, qin_ref, qe_ref, ke_ref, kd_ref, dl_ref, gate_ref, oacc_ref,
     upd_ref) = refs[7 + 2 * n_cast:]
    _cast_blocks(refs[6:6 + n_cast], refs[7 + n_cast:7 + 2 * n_cast])
    _hgrn_body(q_ref, f_ref, i_ref, g_ref, lbl_ref, ong_ref, o_ref, st_ref, b_ref, k_ref, qin_ref,
               qe_ref, ke_ref, kd_ref, dl_ref, gate_ref, oacc_ref, upd_ref, layer)


def _hgrn_body(q_ref, f_ref, i_ref, g_ref, lbl_ref, ong_ref, o_ref,
               st_ref, b_ref, k_ref, qin_ref, qe_ref, ke_ref, kd_ref, dl_ref, gate_ref, oacc_ref, upd_ref,
               layer):
    bt = q_ref.shape[0]
    c_len = HG_CHUNK
    half = c_len // 2
    n_chunks = bt // c_len

    @pl.when(pl.program_id(1) == 0)
    def _():
        st_ref[...] = jnp.zeros_like(st_ref)

    lg = lbl_ref[...]
    e = jnp.exp(lg - jnp.max(lg, axis=0, keepdims=True))
    sm = e / jnp.sum(e, axis=0, keepdims=True)
    cs = sm[0:1]
    for l in range(1, layer + 1):
        cs = cs + sm[l:l + 1]
    lb = cs - sm[0:1]

    sig = 1.0 / (1.0 + jnp.exp(-f_ref[...]))
    fgate = lb + (1.0 - lb) * sig
    kk = 1.0 - fgate
    k_ref[...] = kk
    logf = jnp.log(fgate)
    row = lax.broadcasted_iota(jnp.int32, (c_len, HG_HEAD_DIM), 0)
    shifts = [1 << s for s in range(c_len.bit_length() - 1)]
    keep = [row >= shift for shift in shifts]

    for c in range(n_chunks):
        rows = slice(c * c_len, (c + 1) * c_len)
        bc, kc, q = logf[rows], kk[rows], q_ref[rows, :]
        for shift, mask in zip(shifts, keep):
            bc = bc + jnp.where(mask, pltpu.roll(bc, shift, axis=0), 0.0)
        b_ref[rows, :] = bc
        bl = bc[c_len - 1:c_len, :]
        w = bc - bc[half - 1:half, :]
        qin_ref[rows, :] = (q * jnp.exp(bc)).astype(BF16)
        qe_ref[rows, :] = (q * jnp.exp(w)).astype(BF16)
        ke_ref[rows, :] = (kc * jnp.exp(-w)).astype(BF16)
        kd_ref[rows, :] = (kc * jnp.exp(bl - bc)).astype(BF16)
        dl_ref[c:c + 1, :] = jnp.exp(bl)
    gg = g_ref[...]
    gate_ref[...] = ong_ref[...] * gg / (1.0 + jnp.exp(-gg))

    mids = b_ref[pl.ds(half - 1, bt // half, stride=half), :]
    second = (lax.broadcasted_iota(jnp.int32, mids.shape, 0) & 1) == 1
    safe = jnp.max(jnp.where(second, pltpu.roll(mids, 1, axis=0), 0.0) - mids) <= HG_SAFE_DECAY

    tri = (lax.broadcasted_iota(jnp.int32, (c_len, c_len), 1)
           <= lax.broadcasted_iota(jnp.int32, (c_len, c_len), 0))
    rowc = lax.broadcasted_iota(jnp.int32, (c_len, 1), 0)

    def state_increment(sl):
        return jnp.dot(i_ref[sl, :].T.astype(BF16), kd_ref[sl, :], preferred_element_type=F32)

    def finish():
        o = oacc_ref[...]
        on = o * lax.rsqrt(jnp.mean(o * o, axis=-1, keepdims=True) + NORM_EPS)
        o_ref[...] = (on * gate_ref[...]).astype(BF16)

    @pl.when(safe)
    def _():
        sls = [slice(c * c_len, (c + 1) * c_len) for c in range(n_chunks)]
        atts = [lax.dot_general(qe_ref[sl, :], ke_ref[sl, :], DN_LAST, preferred_element_type=F32)
                for sl in sls]
        atts = [jnp.where(tri, att, 0.0).astype(BF16) for att in atts]
        for sl, att in zip(sls, atts):
            oacc_ref[sl, :] = jnp.dot(att, i_ref[sl, :].astype(BF16), preferred_element_type=F32)
        for c, sl in enumerate(sls):
            upd_ref[c] = state_increment(sl)
        st = st_ref[...]
        for c, sl in enumerate(sls):
            oacc_ref[sl, :] += lax.dot_general(qin_ref[sl, :], st.astype(BF16), DN_LAST,
                                               preferred_element_type=F32)
            st = st * dl_ref[c:c + 1, :] + upd_ref[c]
        st_ref[...] = st
        finish()

    @pl.when(jnp.logical_not(safe))
    def _():
        def body(c, carry):
            r0 = pl.multiple_of(c * c_len, c_len)
            sl = pl.ds(r0, c_len)
            st = st_ref[...]
            q, bc = q_ref[sl, :], b_ref[sl, :]

            def key_row(s, acc):
                bs = b_ref[pl.ds(r0 + s, 1), :]
                ks = k_ref[pl.ds(r0 + s, 1), :]
                vs = i_ref[pl.ds(r0 + s, 1), :]
                a = jnp.sum(q * ks * jnp.exp(jnp.minimum(bc - bs, 0.0)), axis=-1, keepdims=True)
                return acc + jnp.where(rowc >= s, a, 0.0) * vs

            o = lax.dot_general(qin_ref[sl, :], st.astype(BF16), DN_LAST, preferred_element_type=F32)
            oacc_ref[sl, :] = o + lax.fori_loop(0, c_len, key_row, jnp.zeros((c_len, HG_HEAD_DIM), F32))
            st_ref[...] = st * dl_ref[pl.ds(c, 1), :] + state_increment(sl)
            return carry
        lax.fori_loop(0, n_chunks, body, 0)
        finish()


def _hgrn_scan(proj, lb_logits, out_norm_g, layer, cast_jobs=()):
    t = proj.shape[0]
    d = proj.shape[1] // 4
    heads = d // HG_HEAD_DIM
    n_a = lb_logits.shape[0]
    bt = 1024

    def col(part):
        return pl.BlockSpec((bt, HG_HEAD_DIM), lambda h, i: (i, part * heads + h))

    def rows(dtype):
        return pltpu.VMEM((bt, HG_HEAD_DIM), dtype)

    n_row = t // bt
    cast_in, cast_out, cast_shape = _cast_specs(cast_jobs, heads * n_row, lambda h, i: h * n_row + i)
    outs = pl.pallas_call(
        functools.partial(_hgrn_kernel, layer=layer, n_cast=len(cast_jobs)),
        grid=(heads, n_row),
        in_specs=[col(0), col(1), col(2), col(3),
                  pl.BlockSpec((n_a, HG_HEAD_DIM), lambda h, i: (0, h)),
                  pl.BlockSpec((None, 1, HG_HEAD_DIM), lambda h, i: (layer, 0, h))] + cast_in,
        out_specs=[pl.BlockSpec((bt, HG_HEAD_DIM), lambda h, i: (i, h))] + cast_out,
        out_shape=[jax.ShapeDtypeStruct((t, d), BF16)] + cast_shape,
        scratch_shapes=[pltpu.VMEM((HG_HEAD_DIM, HG_HEAD_DIM), F32),
                        rows(F32), rows(F32),
                        rows(BF16), rows(BF16), rows(BF16), rows(BF16),
                        pltpu.VMEM((bt // HG_CHUNK, HG_HEAD_DIM), F32),
                        rows(F32), rows(F32),
                        pltpu.VMEM((bt // HG_CHUNK, HG_HEAD_DIM, HG_HEAD_DIM), F32)],
        compiler_params=_cparams("parallel", "arbitrary"),
        name="hgrn_scan",
    )(proj, proj, proj, proj, lb_logits, out_norm_g.reshape(n_a, 1, d), *[src for src, _ in cast_jobs])
    return outs[0], [w[None] for w in outs[1:]]


def _attn_kernel(sink_ref, bias_ref, q_ref, kc_ref, kp_ref, vc_ref, vp_ref, o_ref):
    jj = pl.program_id(1)
    heads_per_kv = q_ref.shape[1] // (2 * ATT_HEAD_DIM)
    n_slab = heads_per_kv // 2

    k2 = jnp.concatenate([kp_ref[...], kc_ref[...]], axis=0)
    v2 = jnp.concatenate([vp_ref[...], vc_ref[...]], axis=0)
    k2r, v2r = pltpu.roll(k2, ATT_HEAD_DIM, axis=1), pltpu.roll(v2, ATT_HEAD_DIM, axis=1)
    low = lax.broadcasted_iota(jnp.int32, k2.shape, 1) < ATT_HEAD_DIM
    ones_lo = jnp.where(low, 1.0, 0.0)
    real_key = lax.broadcasted_iota(jnp.int32, k2.shape, 0) > 0
    v_low, v_high = low & real_key, jnp.logical_not(low) & real_key
    sink_col = lax.broadcasted_iota(jnp.int32, (WINDOW, LANES), 1) == 0
    bias = bias_ref[...]

    scores, v_exts = [], []
    for kvh in range(2):
        k_lo, k_hi = (k2, k2r) if kvh == 0 else (k2r, k2)
        v_lo, v_hi = (v2, v2r) if kvh == 0 else (v2r, v2)
        k_half = (jnp.where(low, k_lo, 0.0).astype(BF16), jnp.where(low, 0.0, k_hi).astype(BF16))
        v_exts.append((jnp.concatenate([jnp.where(v_low, v_lo, 0.0), ones_lo], axis=1).astype(BF16),
                       jnp.concatenate([jnp.where(v_high, v_hi, 0.0), 1.0 - ones_lo], axis=1).astype(BF16)))
        col0 = kvh * n_slab * LANES
        qs = jnp.concatenate([q_ref[:, col0 + p * LANES:col0 + (p + 1) * LANES] for p in range(n_slab)],
                             axis=0)
        scores.append([lax.dot_general(qs, k_half[parity], DN_LAST, preferred_element_type=F32)
                       for parity in range(2)])

    probs = []
    for kvh in range(2):
        pe = []
        for parity in range(2):
            head0 = (2 * jj + kvh) * heads_per_kv + parity
            slabs = []
            for p in range(n_slab):
                sp = scores[kvh][parity][p * WINDOW:(p + 1) * WINDOW, :] + bias
                sp = jnp.concatenate([jnp.where(sink_col, sink_ref[head0 + 2 * p], sp[:, :LANES]),
                                      sp[:, LANES:]], axis=1)
                slabs.append(jnp.exp(sp - jnp.max(sp, axis=-1, keepdims=True)).astype(BF16))
            pe.append(jnp.concatenate(slabs, axis=0))
        probs.append(pe)

    for kvh in range(2):
        res = [jnp.dot(probs[kvh][parity], v_exts[kvh][parity], preferred_element_type=F32)
               for parity in range(2)]
        out = (res[0][:, :LANES] + res[1][:, :LANES]) / (res[0][:, LANES:] + res[1][:, LANES:])
        col0 = kvh * n_slab * LANES
        for p in range(n_slab):
            o_ref[:, col0 + p * LANES:col0 + (p + 1) * LANES] = out[p * WINDOW:(p + 1) * WINDOW, :].astype(BF16)


def _attention(q, k, v, sinks):
    t, dq = q.shape
    nb = t // WINDOW
    gw = 2 * dq // ATT_KV_HEADS
    a_idx = jnp.arange(WINDOW)[:, None]
    c_idx = jnp.arange(2 * WINDOW)[None, :]
    band = (c_idx >= a_idx + 1) & (c_idx <= a_idx + WINDOW)
    bias = jnp.stack([jnp.where(band & (c_idx >= WINDOW), 0.0, -jnp.inf),
                      jnp.where(band, 0.0, -jnp.inf)]).astype(F32)

    def cur(n, j):
        return (n, j)

    def prev(n, j):
        return (jnp.maximum(n - 1, 0), j)

    return pl.pallas_call(
        _attn_kernel,
        grid=(nb, ATT_KV_HEADS // 2),
        in_specs=[pl.BlockSpec(memory_space=pltpu.SMEM),
                  pl.BlockSpec((None, WINDOW, 2 * WINDOW), lambda n, j: (jnp.minimum(n, 1), 0, 0)),
                  pl.BlockSpec((WINDOW, gw), lambda n, j: (n, j)),
                  pl.BlockSpec((WINDOW, LANES), cur),
                  pl.BlockSpec((WINDOW, LANES), prev),
                  pl.BlockSpec((WINDOW, LANES), cur),
                  pl.BlockSpec((WINDOW, LANES), prev)],
        out_specs=pl.BlockSpec((WINDOW, gw), lambda n, j: (n, j)),
        out_shape=jax.ShapeDtypeStruct((t, dq), BF16),
        compiler_params=_cparams("parallel", "parallel"),
        name="swa_attention",
    )(sinks, bias, q, k, k, v, v)


def kernel(x, p, positions, mixer_norm_g, hgrn_w_in, hgrn_w_out, hgrn_lb_logits, hgrn_out_norm_g,
           kv_norm_g, w_kv, attn_w_q, attn_w_o, attn_sinks, mlp_norm_g, mlp_w1, mlp_w2,
           ple_norm_g, ple_w_gate, ple_w_up, final_norm_g):
    batch, t, d = x.shape
    depth = mlp_w1.shape[0]
    n_a = hgrn_w_in.shape[0]
    assert batch == 1

    w_in, w_kvb, w_up = hgrn_w_in[0:1].astype(BF16), w_kv.astype(BF16)[None], ple_w_up.astype(BF16)
    mixer_w = [hgrn_w_out[0:1].astype(BF16)] if n_a > 0 else [attn_w_q[0:1].astype(BF16),
                                                              attn_w_o[0:1].astype(BF16)]
    if n_a == 0:
        w1, w2 = mlp_w1[0:1].astype(BF16), mlp_w2[0:1].astype(BF16)

    cos, sin = _rope_tables(positions)
    h = x.reshape(t, d)
    hb, ss = _prep_stream(h, mixer_norm_g[0])
    k_sh = v_sh = None

    for layer in range(depth):
        mlp_gain = mlp_norm_g[layer][None, :]
        if layer < n_a:
            jobs = [(mlp_w1, 0), (mlp_w2, 0)] if layer == 0 else []
            proj, cast = _proj(hb, ss, 0, w_in, 0, F32, "hgrn_in_proj", jobs)
            if layer == 0:
                w1, w2 = cast
            jobs = [(hgrn_w_in, layer + 1)] if layer + 1 < n_a else []
            mixed, cast = _hgrn_scan(proj, hgrn_lb_logits, hgrn_out_norm_g, layer, jobs)
            if jobs:
                w_in = cast[0]
            h, hb, ss = _resid_matmul(mixed, mixer_w[0], 0, h, mlp_gain, "hgrn_out_proj")
        else:
            j = layer - n_a
            q = _proj_rope(hb, ss, 0, mixer_w[0], 0, cos, sin, ATT_HEAD_DIM ** -0.5)
            mixed = _attention(q, k_sh, v_sh, attn_sinks[j])
            h, hb, ss = _resid_matmul(mixed, mixer_w[1], 0, h, mlp_gain, "attn_out_proj")

        jobs = [(ple_w_gate, layer)]
        if layer + 1 < depth:
            jobs += [(mlp_w1, layer + 1), (mlp_w2, layer + 1)]
            jobs += ([(hgrn_w_out, layer + 1)] if layer + 1 < n_a
                     else [(attn_w_q, layer + 1 - n_a), (attn_w_o, layer + 1 - n_a)])
        (h, hb, ss), cast = _mlp(hb, ss, w1, w2, 0, h, ple_norm_g[layer][None, :], jobs)
        w_gate = cast[0]
        if layer + 1 < depth:
            w1, w2, mixer_w = cast[1], cast[2], cast[3:]

        if layer + 1 < depth:
            next_gains = [mixer_norm_g[layer + 1]]
            if layer == n_a - 1:
                next_gains.append(kv_norm_g)
        else:
            next_gains = [final_norm_g]
        h, hb, ss = _ple(hb, ss, w_gate, 0, p, w_up, layer, h, jnp.stack(next_gains))

        if layer == n_a - 1:
            k_sh, v_sh = _kv_proj(hb, ss, 1, w_kvb, cos, sin)

    return _final_norm(h, ss, final_norm_g).reshape(batch, t, d)
```

```python
import functools

import jax
import jax.numpy as jnp
from jax import lax
from jax.experimental import pallas as pl
from jax.experimental.pallas import tpu as pltpu

F32 = jnp.float32
BF16 = jnp.bfloat16

LANES = 128
MXU_COLS = 256
VMEM_LIMIT_BYTES = 56 * 1024 * 1024
BIG_VMEM_LIMIT_BYTES = 63 * 1024 * 1024

HG_HEAD_DIM = 128
HG_CHUNK = 64
HG_SAFE_DECAY = 60.0
ATT_HEAD_DIM = 64
ATT_KV_HEADS = 8
WINDOW = 128
ROPE_THETA = 10000.0
NORM_EPS = 1e-6

DN_LAST = (((1,), (1,)), ((), ()))


def _cparams(*sem):
    return pltpu.CompilerParams(dimension_semantics=sem, vmem_limit_bytes=VMEM_LIMIT_BYTES)


def _rstd(ss_ref, d_model):
    return lax.rsqrt(jnp.sum(ss_ref[...], axis=-1, keepdims=True) * (1.0 / d_model) + NORM_EPS)


def _lane_partial_sumsq(x):
    sq = x * x
    acc = sq[:, 0:LANES]
    for c in range(1, x.shape[1] // LANES):
        acc = acc + sq[:, c * LANES:(c + 1) * LANES]
    return acc


def _col_chunks(width):
    return [slice(c, c + MXU_COLS) for c in range(0, width, MXU_COLS)]


def _emit_cols(h_new, cs, g_ref, hn_ref, hb_ref):
    hn_ref[:, cs] = h_new
    for i in range(hb_ref.shape[0]):
        hb_ref[i, :, cs] = (h_new * g_ref[i:i + 1, cs]).astype(BF16)
    return _lane_partial_sumsq(h_new)


def _accumulate_ss(ss_ref, part, first):
    @pl.when(first)
    def _():
        ss_ref[...] = jnp.zeros_like(ss_ref)
    ss_ref[...] += part


def _cast_specs(jobs, n_steps, step_of):
    in_specs, out_specs, out_shape = [], [], []
    for src, layer in jobs:
        _, k, n = src.shape
        rows = k // n_steps
        assert rows * n_steps == k and rows % 16 == 0
        in_specs.append(pl.BlockSpec((None, rows, n), lambda *g, layer=layer: (layer, step_of(*g), 0)))
        out_specs.append(pl.BlockSpec((rows, n), lambda *g: (step_of(*g), 0)))
        out_shape.append(jax.ShapeDtypeStruct((k, n), BF16))
    return in_specs, out_specs, out_shape


def _cast_blocks(src_refs, dst_refs):
    for src, dst in zip(src_refs, dst_refs):
        dst[...] = src[...].astype(BF16)


def _swap_half_heads(x):
    lane = lax.broadcasted_iota(jnp.int32, x.shape, 1)
    first_half = (lane & (ATT_HEAD_DIM - 1)) < (ATT_HEAD_DIM // 2)
    return jnp.where(first_half, pltpu.roll(x, LANES - ATT_HEAD_DIM // 2, axis=1),
                     pltpu.roll(x, ATT_HEAD_DIM // 2, axis=1))


def _rope_store(x, cos, sin_signed, o_ref, col0):
    for c in range(x.shape[1] // LANES):
        xc = x[:, c * LANES:(c + 1) * LANES]
        piece = xc * cos + _swap_half_heads(xc) * sin_signed
        o_ref[:, col0 + c * LANES:col0 + (c + 1) * LANES] = piece.astype(o_ref.dtype)


def _rope_table_kernel(pos_ref, invf_ref, sign_ref, cos_ref, sin_ref):
    ang = pos_ref[...].astype(F32) * invf_ref[...]
    cos_ref[...] = jnp.cos(ang)
    sin_ref[...] = jnp.sin(ang) * sign_ref[...]


def _rope_tables(positions):
    t = positions.shape[-1]
    half = ATT_HEAD_DIM // 2
    inv_freq = ROPE_THETA ** (-jnp.arange(half, dtype=F32) / half)
    invf = jnp.tile(inv_freq, LANES // half)[None, :]
    sign = jnp.tile(jnp.concatenate([-jnp.ones((half,), F32), jnp.ones((half,), F32)]),
                    LANES // ATT_HEAD_DIM)[None, :]
    bt = 1024
    return pl.pallas_call(
        _rope_table_kernel,
        grid=(t // bt,),
        in_specs=[pl.BlockSpec((bt, 1), lambda i: (i, 0)),
                  pl.BlockSpec((1, LANES), lambda i: (0, 0)),
                  pl.BlockSpec((1, LANES), lambda i: (0, 0))],
        out_specs=[pl.BlockSpec((bt, LANES), lambda i: (i, 0))] * 2,
        out_shape=[jax.ShapeDtypeStruct((t, LANES), F32)] * 2,
        compiler_params=_cparams("parallel"),
        name="rope_tables",
    )(positions.reshape(t, 1), invf, sign)


def _prep_kernel(x_ref, g_ref, hb_ref, ss_ref):
    x = x_ref[...]
    hb_ref[0] = (x * g_ref[...]).astype(BF16)
    ss_ref[...] = _lane_partial_sumsq(x)


def _prep_stream(x, g):
    t, d = x.shape
    bt = 256
    return pl.pallas_call(
        _prep_kernel,
        grid=(t // bt,),
        in_specs=[pl.BlockSpec((bt, d), lambda i: (i, 0)),
                  pl.BlockSpec((1, d), lambda i: (0, 0))],
        out_specs=[pl.BlockSpec((1, bt, d), lambda i: (0, i, 0)),
                   pl.BlockSpec((bt, LANES), lambda i: (i, 0))],
        out_shape=[jax.ShapeDtypeStruct((1, t, d), BF16),
                   jax.ShapeDtypeStruct((t, LANES), F32)],
        compiler_params=_cparams("parallel"),
        name="prep_stream",
    )(x, g.reshape(1, d))


def _final_norm_kernel(h_ref, ss_ref, g_ref, o_ref):
    o_ref[...] = h_ref[...] * _rstd(ss_ref, h_ref.shape[1]) * g_ref[...]


def _final_norm(h, ss, g):
    t, d = h.shape
    bt = 256
    return pl.pallas_call(
        _final_norm_kernel,
        grid=(t // bt,),
        in_specs=[pl.BlockSpec((bt, d), lambda i: (i, 0)),
                  pl.BlockSpec((bt, LANES), lambda i: (i, 0)),
                  pl.BlockSpec((1, d), lambda i: (0, 0))],
        out_specs=pl.BlockSpec((bt, d), lambda i: (i, 0)),
        out_shape=jax.ShapeDtypeStruct((t, d), F32),
        compiler_params=_cparams("parallel"),
        name="final_norm",
    )(h, ss, g.reshape(1, d))


def _proj_kernel(*refs, n_cast):
    a_ref, ss_ref, w_ref = refs[:3]
    o_ref = refs[3 + n_cast]
    acc = jnp.dot(a_ref[...], w_ref[...], preferred_element_type=F32)
    o_ref[...] = (acc * _rstd(ss_ref, a_ref.shape[1])).astype(o_ref.dtype)
    _cast_blocks(refs[3:3 + n_cast], refs[4 + n_cast:])


def _proj_rope_kernel(a_ref, ss_ref, w_ref, cos_ref, sin_ref, o_ref, *, scale):
    row_scale = _rstd(ss_ref, a_ref.shape[1]) * scale
    cos, sin = cos_ref[...], sin_ref[...]
    for cs in _col_chunks(o_ref.shape[1]):
        acc = jnp.dot(a_ref[...], w_ref[:, cs], preferred_element_type=F32)
        _rope_store(acc * row_scale, cos, sin, o_ref, cs.start)


def _kv_kernel(a_ref, ss_ref, w_ref, cos_ref, sin_ref, k_ref, v_ref):
    rstd = _rstd(ss_ref, a_ref.shape[1])
    nk = k_ref.shape[1]
    cos, sin = cos_ref[...], sin_ref[...]
    for cs in _col_chunks(nk):
        acc = jnp.dot(a_ref[...], w_ref[:, cs], preferred_element_type=F32)
        _rope_store(acc * rstd, cos, sin, k_ref, cs.start)
    for cs in _col_chunks(nk):
        acc = jnp.dot(a_ref[...], w_ref[:, nk + cs.start:nk + cs.stop], preferred_element_type=F32)
        v_ref[:, cs] = acc * rstd


def _stream_specs(hb, idx, bm, single_buffer=False):
    k = hb.shape[2]
    mode = dict(pipeline_mode=pl.Buffered(1)) if single_buffer else {}
    return [pl.BlockSpec((None, bm, k), lambda m, n: (idx, m, 0), **mode),
            pl.BlockSpec((bm, LANES), lambda m, n: (m, 0))]


def _weight_spec(w, layer, bn):
    return pl.BlockSpec((None, w.shape[1], bn), lambda m, n: (layer, 0, n))


def _proj(hb, ss, idx, w, layer, out_dtype, name, cast_jobs=()):
    _, t, _ = hb.shape
    n = w.shape[2]
    bm, bn = 1024, 1024
    n_col = n // bn
    cast_in, cast_out, cast_shape = _cast_specs(cast_jobs, (t // bm) * n_col, lambda m, n: m * n_col + n)
    outs = pl.pallas_call(
        functools.partial(_proj_kernel, n_cast=len(cast_jobs)),
        grid=(t // bm, n_col),
        in_specs=_stream_specs(hb, idx, bm) + [_weight_spec(w, layer, bn)] + cast_in,
        out_specs=[pl.BlockSpec((bm, bn), lambda m, n: (m, n))] + cast_out,
        out_shape=[jax.ShapeDtypeStruct((t, n), out_dtype)] + cast_shape,
        compiler_params=pltpu.CompilerParams(dimension_semantics=("parallel", "parallel"),
                                             vmem_limit_bytes=BIG_VMEM_LIMIT_BYTES),
        name=name,
    )(hb, ss, w, *[src for src, _ in cast_jobs])
    return outs[0], [w[None] for w in outs[1:]]


def _proj_rope(hb, ss, idx, w, layer, cos, sin, scale):
    _, t, _ = hb.shape
    n = w.shape[2]
    bm, bn = 1024, 1024
    return pl.pallas_call(
        functools.partial(_proj_rope_kernel, scale=scale),
        grid=(t // bm, n // bn),
        in_specs=_stream_specs(hb, idx, bm) + [
            _weight_spec(w, layer, bn),
            pl.BlockSpec((bm, LANES), lambda m, n: (m, 0)),
            pl.BlockSpec((bm, LANES), lambda m, n: (m, 0))],
        out_specs=pl.BlockSpec((bm, bn), lambda m, n: (m, n)),
        out_shape=jax.ShapeDtypeStruct((t, n), BF16),
        compiler_params=_cparams("parallel", "parallel"),
        name="q_proj_rope",
    )(hb, ss, w, cos, sin)


def _kv_proj(hb, ss, idx, w, cos, sin):
    _, t, _ = hb.shape
    n = w.shape[2]
    nk = n // 2
    bm = 1024
    return pl.pallas_call(
        _kv_kernel,
        grid=(t // bm, 1),
        in_specs=_stream_specs(hb, idx, bm) + [
            _weight_spec(w, 0, n),
            pl.BlockSpec((bm, LANES), lambda m, n: (m, 0)),
            pl.BlockSpec((bm, LANES), lambda m, n: (m, 0))],
        out_specs=[pl.BlockSpec((bm, nk), lambda m, n: (m, 0))] * 2,
        out_shape=[jax.ShapeDtypeStruct((t, nk), F32)] * 2,
        compiler_params=_cparams("parallel", "arbitrary"),
        name="kv_proj_rope",
    )(hb, ss, w, cos, sin)


def _resid_kernel(a_ref, w_ref, h_ref, g_ref, hn_ref, hb_ref, ss_ref):
    part = None
    for cs in _col_chunks(hn_ref.shape[1]):
        acc = jnp.dot(a_ref[...], w_ref[:, cs], preferred_element_type=F32)
        p = _emit_cols(h_ref[:, cs] + acc, cs, g_ref, hn_ref, hb_ref)
        part = p if part is None else part + p
    _accumulate_ss(ss_ref, part, pl.program_id(1) == 0)


def _ple_kernel(a_ref, ssin_ref, w_ref, p_ref, wup_ref, h_ref, g_ref, hn_ref, hb_ref, ss_ref):
    rstd = _rstd(ssin_ref, a_ref.shape[1])
    pb = p_ref[...].astype(BF16)
    part = None
    for cs in _col_chunks(hn_ref.shape[1]):
        z = jnp.dot(a_ref[...], w_ref[:, cs], preferred_element_type=F32) * rstd
        gate = 1.0 / (1.0 + jnp.exp(-z))
        up = jnp.dot(pb, wup_ref[:, cs], preferred_element_type=F32)
        p = _emit_cols(h_ref[:, cs] + gate * up, cs, g_ref, hn_ref, hb_ref)
        part = p if part is None else part + p
    _accumulate_ss(ss_ref, part, pl.program_id(1) == 0)


def _stream_out(t, d, ng, bm, bn):
    specs = [pl.BlockSpec((bm, bn), lambda m, n: (m, n)),
             pl.BlockSpec((ng, bm, bn), lambda m, n: (0, m, n)),
             pl.BlockSpec((bm, LANES), lambda m, n: (m, 0))]
    shapes = [jax.ShapeDtypeStruct((t, d), F32),
              jax.ShapeDtypeStruct((ng, t, d), BF16),
              jax.ShapeDtypeStruct((t, LANES), F32)]
    return specs, shapes


def _resid_matmul(a, w, layer, h, gains, name):
    t, k = a.shape
    d = w.shape[2]
    ng = gains.shape[0]
    bm, bn = 1024, 512
    out_specs, out_shape = _stream_out(t, d, ng, bm, bn)
    return pl.pallas_call(
        _resid_kernel,
        grid=(t // bm, d // bn),
        in_specs=[pl.BlockSpec((bm, k), lambda m, n: (m, 0)),
                  _weight_spec(w, layer, bn),
                  pl.BlockSpec((bm, bn), lambda m, n: (m, n)),
                  pl.BlockSpec((ng, bn), lambda m, n: (0, n))],
        out_specs=out_specs,
        out_shape=out_shape,
        compiler_params=_cparams("parallel", "arbitrary"),
        name=name,
    )(a, w, h, gains)


def _ple(hb, ss, w_gate, gate_layer, p, w_up, layer, h, gains):
    _, t, _ = hb.shape
    d = w_gate.shape[2]
    ng = gains.shape[0]
    pdim = p.shape[-1]
    bm, bn = 1024, 512
    out_specs, out_shape = _stream_out(t, d, ng, bm, bn)
    return pl.pallas_call(
        _ple_kernel,
        grid=(t // bm, d // bn),
        in_specs=_stream_specs(hb, 0, bm) + [
            _weight_spec(w_gate, gate_layer, bn),
            pl.BlockSpec((None, None, bm, pdim), lambda m, n: (layer, 0, m, 0)),
            _weight_spec(w_up, layer, bn),
            pl.BlockSpec((bm, bn), lambda m, n: (m, n)),
            pl.BlockSpec((ng, bn), lambda m, n: (0, n))],
        out_specs=out_specs,
        out_shape=out_shape,
        compiler_params=_cparams("parallel", "arbitrary"),
        name="ple",
    )(hb, ss, w_gate, p, w_up, h, gains)


def _mlp_kernel(*refs, n_ff, n_cast):
    a_ref, ssin_ref, w1_ref, w2_ref, h_ref, g_ref = refs[:6]
    cast_src = refs[6:6 + n_cast]
    hn_ref, hb_ref, ss_ref = refs[6 + n_cast:9 + n_cast]
    cast_dst = refs[9 + n_cast:9 + 2 * n_cast]
    acc_ref = refs[-1]
    f = pl.program_id(1)
    ec = acc_ref.shape[2]

    @pl.when(f == 0)
    def _():
        acc_ref[...] = jnp.zeros_like(acc_ref)

    @pl.when(f < n_ff)
    def _():
        z = jnp.dot(a_ref[...], w1_ref[...], preferred_element_type=F32)
        z = jnp.maximum(z * _rstd(ssin_ref, a_ref.shape[1]), 0.0)
        act = (z * z).astype(BF16)
        for j in range(acc_ref.shape[0]):
            acc_ref[j] += jnp.dot(act, w2_ref[:, j * ec:(j + 1) * ec], preferred_element_type=F32)
        _cast_blocks(cast_src, cast_dst)

    @pl.when(f >= n_ff)
    def _():
        part = _emit_cols(h_ref[...] + acc_ref[f - n_ff], slice(0, ec), g_ref, hn_ref, hb_ref)
        _accumulate_ss(ss_ref, part, f == n_ff)


def _mlp(hb, ss, w1, w2, layer, h, gains, cast_jobs=()):
    _, t, k = hb.shape
    d_ff = w1.shape[2]
    d = w2.shape[2]
    ng = gains.shape[0]
    bm, bf = 1024, 512
    ec = 512
    n_ff, n_ec = d_ff // bf, d // ec
    last = n_ff - 1

    def epi(m, f):
        return (m, jnp.maximum(f - n_ff, 0))

    cast_in, cast_out, cast_shape = _cast_specs(cast_jobs, (t // bm) * n_ff,
                                                lambda m, f: m * n_ff + jnp.minimum(f, last))
    outs = pl.pallas_call(
        functools.partial(_mlp_kernel, n_ff=n_ff, n_cast=len(cast_jobs)),
        grid=(t // bm, n_ff + n_ec),
        in_specs=[
            pl.BlockSpec((None, bm, k), lambda m, f: (0, m, 0), pipeline_mode=pl.Buffered(1)),
            pl.BlockSpec((bm, LANES), lambda m, f: (m, 0)),
            pl.BlockSpec((None, k, bf), lambda m, f: (layer, 0, jnp.minimum(f, last))),
            pl.BlockSpec((None, bf, d), lambda m, f: (layer, jnp.minimum(f, last), 0)),
            pl.BlockSpec((bm, ec), epi),
            pl.BlockSpec((ng, ec), lambda m, f: (0, jnp.maximum(f - n_ff, 0)))] + cast_in,
        out_specs=[pl.BlockSpec((bm, ec), epi),
                   pl.BlockSpec((ng, bm, ec), lambda m, f: (0, m, jnp.maximum(f - n_ff, 0))),
                   pl.BlockSpec((bm, LANES), lambda m, f: (m, 0))] + cast_out,
        out_shape=[jax.ShapeDtypeStruct((t, d), F32),
                   jax.ShapeDtypeStruct((ng, t, d), BF16),
                   jax.ShapeDtypeStruct((t, LANES), F32)] + cast_shape,
        scratch_shapes=[pltpu.VMEM((n_ec, bm, ec), F32)],
        compiler_params=pltpu.CompilerParams(dimension_semantics=("parallel", "arbitrary"),
                                             vmem_limit_bytes=BIG_VMEM_LIMIT_BYTES),
        name="mlp",
    )(hb, ss, w1, w2, h, gains, *[src for src, _ in cast_jobs])
    return outs[:3], [w[None] for w in outs[3:]]


def _hgrn_kernel(*refs, layer, n_cast):
    q_ref, f_ref, i_ref, g_ref, lbl_ref, ong_ref = refs[:6]
    o_ref = refs[6 + n_cast]
    (st_ref, b_ref, k_ref, qin_ref, qe_ref, ke_ref, kd_ref, dl_ref, gate_ref, oacc_ref,
     upd_ref) = refs[7 + 2 * n_cast:]
    _cast_blocks(refs[6:6 + n_cast], refs[7 + n_cast:7 + 2 * n_cast])
    _hgrn_body(q_ref, f_ref, i_ref, g_ref, lbl_ref, ong_ref, o_ref, st_ref, b_ref, k_ref, qin_ref,
               qe_ref, ke_ref, kd_ref, dl_ref, gate_ref, oacc_ref, upd_ref, layer)


def _hgrn_body(q_ref, f_ref, i_ref, g_ref, lbl_ref, ong_ref, o_ref,
               st_ref, b_ref, k_ref, qin_ref, qe_ref, ke_ref, kd_ref, dl_ref, gate_ref, oacc_ref, upd_ref,
               layer):
    bt = q_ref.shape[0]
    c_len = HG_CHUNK
    half = c_len // 2
    n_chunks = bt // c_len

    @pl.when(pl.program_id(1) == 0)
    def _():
        st_ref[...] = jnp.zeros_like(st_ref)

    lg = lbl_ref[...]
    e = jnp.exp(lg - jnp.max(lg, axis=0, keepdims=True))
    sm = e / jnp.sum(e, axis=0, keepdims=True)
    cs = sm[0:1]
    for l in range(1, layer + 1):
        cs = cs + sm[l:l + 1]
    lb = cs - sm[0:1]

    sig = 1.0 / (1.0 + jnp.exp(-f_ref[...]))
    fgate = lb + (1.0 - lb) * sig
    kk = 1.0 - fgate
    k_ref[...] = kk
    logf = jnp.log(fgate)
    row = lax.broadcasted_iota(jnp.int32, (c_len, HG_HEAD_DIM), 0)
    shifts = [1 << s for s in range(c_len.bit_length() - 1)]
    keep = [row >= shift for shift in shifts]

    for c in range(n_chunks):
        rows = slice(c * c_len, (c + 1) * c_len)
        bc, kc, q = logf[rows], kk[rows], q_ref[rows, :]
        for shift, mask in zip(shifts, keep):
            bc = bc + jnp.where(mask, pltpu.roll(bc, shift, axis=0), 0.0)
        b_ref[rows, :] = bc
        bl = bc[c_len - 1:c_len, :]
        w = bc - bc[half - 1:half, :]
        qin_ref[rows, :] = (q * jnp.exp(bc)).astype(BF16)
        qe_ref[rows, :] = (q * jnp.exp(w)).astype(BF16)
        ke_ref[rows, :] = (kc * jnp.exp(-w)).astype(BF16)
        kd_ref[rows, :] = (kc * jnp.exp(bl - bc)).astype(BF16)
        dl_ref[c:c + 1, :] = jnp.exp(bl)
    gg = g_ref[...]
    gate_ref[...] = ong_ref[...] * gg / (1.0 + jnp.exp(-gg))

    mids = b_ref[pl.ds(half - 1, bt // half, stride=half), :]
    second = (lax.broadcasted_iota(jnp.int32, mids.shape, 0) & 1) == 1
    safe = jnp.max(jnp.where(second, pltpu.roll(mids, 1, axis=0), 0.0) - mids) <= HG_SAFE_DECAY

    tri = (lax.broadcasted_iota(jnp.int32, (c_len, c_len), 1)
           <= lax.broadcasted_iota(jnp.int32, (c_len, c_len), 0))
    rowc = lax.broadcasted_iota(jnp.int32, (c_len, 1), 0)

    def state_increment(sl):
        return jnp.dot(i_ref[sl, :].T.astype(BF16), kd_ref[sl, :], preferred_element_type=F32)

    def finish():
        o = oacc_ref[...]
        on = o * lax.rsqrt(jnp.mean(o * o, axis=-1, keepdims=True) + NORM_EPS)
        o_ref[...] = (on * gate_ref[...]).astype(BF16)

    @pl.when(safe)
    def _():
        sls = [slice(c * c_len, (c + 1) * c_len) for c in range(n_chunks)]
        atts = [lax.dot_general(qe_ref[sl, :], ke_ref[sl, :], DN_LAST, preferred_element_type=F32)
                for sl in sls]
        atts = [jnp.where(tri, att, 0.0).astype(BF16) for att in atts]
        for sl, att in zip(sls, atts):
            oacc_ref[sl, :] = jnp.dot(att, i_ref[sl, :].astype(BF16), preferred_element_type=F32)
        for c, sl in enumerate(sls):
            upd_ref[c] = state_increment(sl)
        st = st_ref[...]
        for c, sl in enumerate(sls):
            oacc_ref[sl, :] += lax.dot_general(qin_ref[sl, :], st.astype(BF16), DN_LAST,
                                               preferred_element_type=F32)
            st = st * dl_ref[c:c + 1, :] + upd_ref[c]
        st_ref[...] = st
        finish()

    @pl.when(jnp.logical_not(safe))
    def _():
        def body(c, carry):
            r0 = pl.multiple_of(c * c_len, c_len)
            sl = pl.ds(r0, c_len)
            st = st_ref[...]
            q, bc = q_ref[sl, :], b_ref[sl, :]

            def key_row(s, acc):
                bs = b_ref[pl.ds(r0 + s, 1), :]
                ks = k_ref[pl.ds(r0 + s, 1), :]
                vs = i_ref[pl.ds(r0 + s, 1), :]
                a = jnp.sum(q * ks * jnp.exp(jnp.minimum(bc - bs, 0.0)), axis=-1, keepdims=True)
                return acc + jnp.where(rowc >= s, a, 0.0) * vs

            o = lax.dot_general(qin_ref[sl, :], st.astype(BF16), DN_LAST, preferred_element_type=F32)
            oacc_ref[sl, :] = o + lax.fori_loop(0, c_len, key_row, jnp.zeros((c_len, HG_HEAD_DIM), F32))
            st_ref[...] = st * dl_ref[pl.ds(c, 1), :] + state_increment(sl)
            return carry
        lax.fori_loop(0, n_chunks, body, 0)
        finish()


def _hgrn_scan(proj, lb_logits, out_norm_g, layer, cast_jobs=()):
    t = proj.shape[0]
    d = proj.shape[1] // 4
    heads = d // HG_HEAD_DIM
    n_a = lb_logits.shape[0]
    bt = 2048

    def col(part):
        return pl.BlockSpec((bt, HG_HEAD_DIM), lambda h, i: (i, part * heads + h))

    def rows(dtype):
        return pltpu.VMEM((bt, HG_HEAD_DIM), dtype)

    n_row = t // bt
    cast_in, cast_out, cast_shape = _cast_specs(cast_jobs, heads * n_row, lambda h, i: h * n_row + i)
    outs = pl.pallas_call(
        functools.partial(_hgrn_kernel, layer=layer, n_cast=len(cast_jobs)),
        grid=(heads, n_row),
        in_specs=[col(0), col(1), col(2), col(3),
                  pl.BlockSpec((n_a, HG_HEAD_DIM), lambda h, i: (0, h)),
                  pl.BlockSpec((None, 1, HG_HEAD_DIM), lambda h, i: (layer, 0, h))] + cast_in,
        out_specs=[pl.BlockSpec((bt, HG_HEAD_DIM), lambda h, i: (i, h))] + cast_out,
        out_shape=[jax.ShapeDtypeStruct((t, d), BF16)] + cast_shape,
        scratch_shapes=[pltpu.VMEM((HG_HEAD_DIM, HG_HEAD_DIM), F32),
                        rows(F32), rows(F32),
                        rows(BF16), rows(BF16), rows(BF16), rows(BF16),
                        pltpu.VMEM((bt // HG_CHUNK, HG_HEAD_DIM), F32),
                        rows(F32), rows(F32),
                        pltpu.VMEM((bt // HG_CHUNK, HG_HEAD_DIM, HG_HEAD_DIM), F32)],
        compiler_params=_cparams("parallel", "arbitrary"),
        name="hgrn_scan",
    )(proj, proj, proj, proj, lb_logits, out_norm_g.reshape(n_a, 1, d), *[src for src, _ in cast_jobs])
    return outs[0], [w[None] for w in outs[1:]]


def _attn_kernel(sink_ref, bias_ref, q_ref, kc_ref, kp_ref, vc_ref, vp_ref, o_ref):
    n, jj = pl.program_id(0), pl.program_id(1)
    heads_per_kv = q_ref.shape[1] // (2 * ATT_HEAD_DIM)
    n_slab = heads_per_kv // 2

    low = lax.broadcasted_iota(jnp.int32, (2 * WINDOW, LANES), 1) < ATT_HEAD_DIM
    ones_lo = jnp.where(low, 1.0, 0.0)
    real_key = lax.broadcasted_iota(jnp.int32, (2 * WINDOW, LANES), 0) > 0
    v_low, v_high = low & real_key, jnp.logical_not(low) & real_key
    sink_col = lax.broadcasted_iota(jnp.int32, (WINDOW, LANES), 1) == 0

    for qb in range(q_ref.shape[0] // WINDOW):
        rows = slice(qb * WINDOW, (qb + 1) * WINDOW)
        if qb == 0:
            k_prev, v_prev = kp_ref[...], vp_ref[...]
            bias = jnp.where(n == 0, bias_ref[0], bias_ref[1])
        else:
            before = slice((qb - 1) * WINDOW, qb * WINDOW)
            k_prev, v_prev = kc_ref[before, :], vc_ref[before, :]
            bias = bias_ref[1]
        k2 = jnp.concatenate([k_prev, kc_ref[rows, :]], axis=0)
        v2 = jnp.concatenate([v_prev, vc_ref[rows, :]], axis=0)
        k2r, v2r = pltpu.roll(k2, ATT_HEAD_DIM, axis=1), pltpu.roll(v2, ATT_HEAD_DIM, axis=1)

        scores, v_exts = [], []
        for kvh in range(2):
            k_lo, k_hi = (k2, k2r) if kvh == 0 else (k2r, k2)
            v_lo, v_hi = (v2, v2r) if kvh == 0 else (v2r, v2)
            k_half = (jnp.where(low, k_lo, 0.0).astype(BF16), jnp.where(low, 0.0, k_hi).astype(BF16))
            v_exts.append((jnp.concatenate([jnp.where(v_low, v_lo, 0.0), ones_lo], axis=1).astype(BF16),
                           jnp.concatenate([jnp.where(v_high, v_hi, 0.0), 1.0 - ones_lo], axis=1).astype(BF16)))
            col0 = kvh * n_slab * LANES
            qs = jnp.concatenate([q_ref[rows, col0 + p * LANES:col0 + (p + 1) * LANES] for p in range(n_slab)],
                                 axis=0)
            scores.append([lax.dot_general(qs, k_half[parity], DN_LAST, preferred_element_type=F32)
                           for parity in range(2)])

        probs = []
        for kvh in range(2):
            pe = []
            for parity in range(2):
                head0 = (2 * jj + kvh) * heads_per_kv + parity
                slabs = []
                for p in range(n_slab):
                    sp = scores[kvh][parity][p * WINDOW:(p + 1) * WINDOW, :] + bias
                    sp = jnp.concatenate([jnp.where(sink_col, sink_ref[head0 + 2 * p], sp[:, :LANES]),
                                          sp[:, LANES:]], axis=1)
                    slabs.append(jnp.exp(sp - jnp.max(sp, axis=-1, keepdims=True)).astype(BF16))
                pe.append(jnp.concatenate(slabs, axis=0))
            probs.append(pe)

        for kvh in range(2):
            res = [jnp.dot(probs[kvh][parity], v_exts[kvh][parity], preferred_element_type=F32)
                   for parity in range(2)]
            out = (res[0][:, :LANES] + res[1][:, :LANES]) / (res[0][:, LANES:] + res[1][:, LANES:])
            col0 = kvh * n_slab * LANES
            for p in range(n_slab):
                o_ref[rows, col0 + p * LANES:col0 + (p + 1) * LANES] = (
                    out[p * WINDOW:(p + 1) * WINDOW, :].astype(BF16))


def _attention(q, k, v, sinks):
    t, dq = q.shape
    qb = 2
    rows = qb * WINDOW
    gw = 2 * dq // ATT_KV_HEADS
    a_idx = jnp.arange(WINDOW)[:, None]
    c_idx = jnp.arange(2 * WINDOW)[None, :]
    band = (c_idx >= a_idx + 1) & (c_idx <= a_idx + WINDOW)
    bias = jnp.stack([jnp.where(band & (c_idx >= WINDOW), 0.0, -jnp.inf),
                      jnp.where(band, 0.0, -jnp.inf)]).astype(F32)

    def cur(n, j):
        return (n, j)

    def prev(n, j):
        return (jnp.maximum(qb * n - 1, 0), j)

    return pl.pallas_call(
        _attn_kernel,
        grid=(t // rows, ATT_KV_HEADS // 2),
        in_specs=[pl.BlockSpec(memory_space=pltpu.SMEM),
                  pl.BlockSpec((2, WINDOW, 2 * WINDOW), lambda n, j: (0, 0, 0)),
                  pl.BlockSpec((rows, gw), lambda n, j: (n, j)),
                  pl.BlockSpec((rows, LANES), cur),
                  pl.BlockSpec((WINDOW, LANES), prev),
                  pl.BlockSpec((rows, LANES), cur),
                  pl.BlockSpec((WINDOW, LANES), prev)],
        out_specs=pl.BlockSpec((rows, gw), lambda n, j: (n, j)),
        out_shape=jax.ShapeDtypeStruct((t, dq), BF16),
        compiler_params=_cparams("parallel", "parallel"),
        name="swa_attention",
    )(sinks, bias, q, k, k, v, v)


def kernel(x, p, positions, mixer_norm_g, hgrn_w_in, hgrn_w_out, hgrn_lb_logits, hgrn_out_norm_g,
           kv_norm_g, w_kv, attn_w_q, attn_w_o, attn_sinks, mlp_norm_g, mlp_w1, mlp_w2,
           ple_norm_g, ple_w_gate, ple_w_up, final_norm_g):
    batch, t, d = x.shape
    depth = mlp_w1.shape[0]
    n_a = hgrn_w_in.shape[0]
    assert batch == 1

    w_in, w_kvb, w_up = hgrn_w_in[0:1].astype(BF16), w_kv.astype(BF16)[None], ple_w_up.astype(BF16)
    mixer_w = [hgrn_w_out[0:1].astype(BF16)] if n_a > 0 else [attn_w_q[0:1].astype(BF16),
                                                              attn_w_o[0:1].astype(BF16)]
    if n_a == 0:
        w1, w2 = mlp_w1[0:1].astype(BF16), mlp_w2[0:1].astype(BF16)

    cos, sin = _rope_tables(positions)
    h = x.reshape(t, d)
    hb, ss = _prep_stream(h, mixer_norm_g[0])
    k_sh = v_sh = None

    for layer in range(depth):
        mlp_gain = mlp_norm_g[layer][None, :]
        if layer < n_a:
            jobs = [(mlp_w1, 0), (mlp_w2, 0)] if layer == 0 else []
            proj, cast = _proj(hb, ss, 0, w_in, 0, F32, "hgrn_in_proj", jobs)
            if layer == 0:
                w1, w2 = cast
            jobs = [(hgrn_w_in, layer + 1)] if layer + 1 < n_a else []
            mixed, cast = _hgrn_scan(proj, hgrn_lb_logits, hgrn_out_norm_g, layer, jobs)
            if jobs:
                w_in = cast[0]
            h, hb, ss = _resid_matmul(mixed, mixer_w[0], 0, h, mlp_gain, "hgrn_out_proj")
        else:
            j = layer - n_a
            q = _proj_rope(hb, ss, 0, mixer_w[0], 0, cos, sin, ATT_HEAD_DIM ** -0.5)
            mixed = _attention(q, k_sh, v_sh, attn_sinks[j])
            h, hb, ss = _resid_matmul(mixed, mixer_w[1], 0, h, mlp_gain, "attn_out_proj")

        jobs = [(ple_w_gate, layer)]
        if layer + 1 < depth:
            jobs += [(mlp_w1, layer + 1), (mlp_w2, layer + 1)]
            jobs += ([(hgrn_w_out, layer + 1)] if layer + 1 < n_a
                     else [(attn_w_q, layer + 1 - n_a), (attn_w_o, layer + 1 - n_a)])
        (h, hb, ss), cast = _mlp(hb, ss, w1, w2, 0, h, ple_norm_g[layer][None, :], jobs)
        w_gate = cast[0]
        if layer + 1 < depth:
            w1, w2, mixer_w = cast[1], cast[2], cast[3:]

        if layer + 1 < depth:
            next_gains = [mixer_norm_g[layer + 1]]
            if layer == n_a - 1:
                next_gains.append(kv_norm_g)
        else:
            next_gains = [final_norm_g]
        h, hb, ss = _ple(hb, ss, w_gate, 0, p, w_up, layer, h, jnp.stack(next_gains))

        if layer == n_a - 1:
            k_sh, v_sh = _kv_proj(hb, ss, 1, w_kvb, cos, sin)

    return _final_norm(h, ss, final_norm_g).reshape(batch, t, d)
```

```python
import functools

import jax
import jax.numpy as jnp
from jax import lax
from jax.experimental import pallas as pl
from jax.experimental.pallas import tpu as pltpu

F32 = jnp.float32
BF16 = jnp.bfloat16

LANES = 128
MXU_COLS = 256
VMEM_LIMIT_BYTES = 56 * 1024 * 1024
BIG_VMEM_LIMIT_BYTES = 63 * 1024 * 1024

HG_HEAD_DIM = 128
HG_CHUNK = 64
HG_SAFE_DECAY = 60.0
ATT_HEAD_DIM = 64
ATT_KV_HEADS = 8
WINDOW = 128
ROPE_THETA = 10000.0
NORM_EPS = 1e-6

DN_LAST = (((1,), (1,)), ((), ()))


def _cparams(*sem):
    return pltpu.CompilerParams(dimension_semantics=sem, vmem_limit_bytes=VMEM_LIMIT_BYTES)


def _rstd(ss_ref, d_model):
    return lax.rsqrt(jnp.sum(ss_ref[...], axis=-1, keepdims=True) * (1.0 / d_model) + NORM_EPS)


def _lane_partial_sumsq(x):
    sq = x * x
    acc = sq[:, 0:LANES]
    for c in range(1, x.shape[1] // LANES):
        acc = acc + sq[:, c * LANES:(c + 1) * LANES]
    return acc


def _col_chunks(width):
    return [slice(c, c + MXU_COLS) for c in range(0, width, MXU_COLS)]


def _emit_cols(h_new, cs, g_ref, hn_ref, hb_ref):
    hn_ref[:, cs] = h_new
    for i in range(hb_ref.shape[0]):
        hb_ref[i, :, cs] = (h_new * g_ref[i:i + 1, cs]).astype(BF16)
    return _lane_partial_sumsq(h_new)


def _accumulate_ss(ss_ref, part, first):
    @pl.when(first)
    def _():
        ss_ref[...] = jnp.zeros_like(ss_ref)
    ss_ref[...] += part


def _cast_specs(jobs, n_steps, step_of):
    in_specs, out_specs, out_shape = [], [], []
    for src, layer in jobs:
        _, k, n = src.shape
        rows = k // n_steps
        assert rows * n_steps == k and rows % 16 == 0
        in_specs.append(pl.BlockSpec((None, rows, n), lambda *g, layer=layer: (layer, step_of(*g), 0)))
        out_specs.append(pl.BlockSpec((rows, n), lambda *g: (step_of(*g), 0)))
        out_shape.append(jax.ShapeDtypeStruct((k, n), BF16))
    return in_specs, out_specs, out_shape


def _cast_blocks(src_refs, dst_refs):
    for src, dst in zip(src_refs, dst_refs):
        dst[...] = src[...].astype(BF16)


def _swap_half_heads(x):
    lane = lax.broadcasted_iota(jnp.int32, x.shape, 1)
    first_half = (lane & (ATT_HEAD_DIM - 1)) < (ATT_HEAD_DIM // 2)
    return jnp.where(first_half, pltpu.roll(x, LANES - ATT_HEAD_DIM // 2, axis=1),
                     pltpu.roll(x, ATT_HEAD_DIM // 2, axis=1))


def _rope_store(x, cos, sin_signed, o_ref, col0):
    for c in range(x.shape[1] // LANES):
        xc = x[:, c * LANES:(c + 1) * LANES]
        piece = xc * cos + _swap_half_heads(xc) * sin_signed
        o_ref[:, col0 + c * LANES:col0 + (c + 1) * LANES] = piece.astype(o_ref.dtype)


def _rope_table_kernel(pos_ref, invf_ref, sign_ref, cos_ref, sin_ref):
    ang = pos_ref[...].astype(F32) * invf_ref[...]
    cos_ref[...] = jnp.cos(ang)
    sin_ref[...] = jnp.sin(ang) * sign_ref[...]


def _rope_tables(positions):
    t = positions.shape[-1]
    half = ATT_HEAD_DIM // 2
    inv_freq = ROPE_THETA ** (-jnp.arange(half, dtype=F32) / half)
    invf = jnp.tile(inv_freq, LANES // half)[None, :]
    sign = jnp.tile(jnp.concatenate([-jnp.ones((half,), F32), jnp.ones((half,), F32)]),
                    LANES // ATT_HEAD_DIM)[None, :]
    bt = 1024
    return pl.pallas_call(
        _rope_table_kernel,
        grid=(t // bt,),
        in_specs=[pl.BlockSpec((bt, 1), lambda i: (i, 0)),
                  pl.BlockSpec((1, LANES), lambda i: (0, 0)),
                  pl.BlockSpec((1, LANES), lambda i: (0, 0))],
        out_specs=[pl.BlockSpec((bt, LANES), lambda i: (i, 0))] * 2,
        out_shape=[jax.ShapeDtypeStruct((t, LANES), F32)] * 2,
        compiler_params=_cparams("parallel"),
        name="rope_tables",
    )(positions.reshape(t, 1), invf, sign)


def _prep_kernel(x_ref, g_ref, hb_ref, ss_ref):
    x = x_ref[...]
    hb_ref[0] = (x * g_ref[...]).astype(BF16)
    ss_ref[...] = _lane_partial_sumsq(x)


def _prep_stream(x, g):
    t, d = x.shape
    bt = 256
    return pl.pallas_call(
        _prep_kernel,
        grid=(t // bt,),
        in_specs=[pl.BlockSpec((bt, d), lambda i: (i, 0)),
                  pl.BlockSpec((1, d), lambda i: (0, 0))],
        out_specs=[pl.BlockSpec((1, bt, d), lambda i: (0, i, 0)),
                   pl.BlockSpec((bt, LANES), lambda i: (i, 0))],
        out_shape=[jax.ShapeDtypeStruct((1, t, d), BF16),
                   jax.ShapeDtypeStruct((t, LANES), F32)],
        compiler_params=_cparams("parallel"),
        name="prep_stream",
    )(x, g.reshape(1, d))


def _final_norm_kernel(h_ref, ss_ref, g_ref, o_ref):
    o_ref[...] = h_ref[...] * _rstd(ss_ref, h_ref.shape[1]) * g_ref[...]


def _final_norm(h, ss, g):
    t, d = h.shape
    bt = 256
    return pl.pallas_call(
        _final_norm_kernel,
        grid=(t // bt,),
        in_specs=[pl.BlockSpec((bt, d), lambda i: (i, 0)),
                  pl.BlockSpec((bt, LANES), lambda i: (i, 0)),
                  pl.BlockSpec((1, d), lambda i: (0, 0))],
        out_specs=pl.BlockSpec((bt, d), lambda i: (i, 0)),
        out_shape=jax.ShapeDtypeStruct((t, d), F32),
        compiler_params=_cparams("parallel"),
        name="final_norm",
    )(h, ss, g.reshape(1, d))


def _proj_kernel(*refs, n_cast):
    a_ref, ss_ref, w_ref = refs[:3]
    o_ref = refs[3 + n_cast]
    acc = jnp.dot(a_ref[...], w_ref[...], preferred_element_type=F32)
    o_ref[...] = (acc * _rstd(ss_ref, a_ref.shape[1])).astype(o_ref.dtype)
    _cast_blocks(refs[3:3 + n_cast], refs[4 + n_cast:])


def _proj_rope_kernel(a_ref, ss_ref, w_ref, cos_ref, sin_ref, o_ref, *, scale):
    row_scale = _rstd(ss_ref, a_ref.shape[1]) * scale
    cos, sin = cos_ref[...], sin_ref[...]
    for cs in _col_chunks(o_ref.shape[1]):
        acc = jnp.dot(a_ref[...], w_ref[:, cs], preferred_element_type=F32)
        _rope_store(acc * row_scale, cos, sin, o_ref, cs.start)


def _kv_kernel(a_ref, ss_ref, w_ref, cos_ref, sin_ref, k_ref, v_ref):
    rstd = _rstd(ss_ref, a_ref.shape[1])
    nk = k_ref.shape[1]
    cos, sin = cos_ref[...], sin_ref[...]
    for cs in _col_chunks(nk):
        acc = jnp.dot(a_ref[...], w_ref[:, cs], preferred_element_type=F32)
        _rope_store(acc * rstd, cos, sin, k_ref, cs.start)
    for cs in _col_chunks(nk):
        acc = jnp.dot(a_ref[...], w_ref[:, nk + cs.start:nk + cs.stop], preferred_element_type=F32)
        v_ref[:, cs] = acc * rstd


def _stream_specs(hb, idx, bm, single_buffer=False):
    k = hb.shape[2]
    mode = dict(pipeline_mode=pl.Buffered(1)) if single_buffer else {}
    return [pl.BlockSpec((None, bm, k), lambda m, n: (idx, m, 0), **mode),
            pl.BlockSpec((bm, LANES), lambda m, n: (m, 0))]


def _weight_spec(w, layer, bn):
    return pl.BlockSpec((None, w.shape[1], bn), lambda m, n: (layer, 0, n))


def _proj(hb, ss, idx, w, layer, out_dtype, name, cast_jobs=()):
    _, t, _ = hb.shape
    n = w.shape[2]
    bm, bn = 1024, 1024
    n_col = n // bn
    cast_in, cast_out, cast_shape = _cast_specs(cast_jobs, (t // bm) * n_col, lambda m, n: m * n_col + n)
    outs = pl.pallas_call(
        functools.partial(_proj_kernel, n_cast=len(cast_jobs)),
        grid=(t // bm, n_col),
        in_specs=_stream_specs(hb, idx, bm) + [_weight_spec(w, layer, bn)] + cast_in,
        out_specs=[pl.BlockSpec((bm, bn), lambda m, n: (m, n))] + cast_out,
        out_shape=[jax.ShapeDtypeStruct((t, n), out_dtype)] + cast_shape,
        compiler_params=pltpu.CompilerParams(dimension_semantics=("parallel", "parallel"),
                                             vmem_limit_bytes=BIG_VMEM_LIMIT_BYTES),
        name=name,
    )(hb, ss, w, *[src for src, _ in cast_jobs])
    return outs[0], [w[None] for w in outs[1:]]


def _proj_rope(hb, ss, idx, w, layer, cos, sin, scale):
    _, t, _ = hb.shape
    n = w.shape[2]
    bm, bn = 1024, 1024
    return pl.pallas_call(
        functools.partial(_proj_rope_kernel, scale=scale),
        grid=(t // bm, n // bn),
        in_specs=_stream_specs(hb, idx, bm) + [
            _weight_spec(w, layer, bn),
            pl.BlockSpec((bm, LANES), lambda m, n: (m, 0)),
            pl.BlockSpec((bm, LANES), lambda m, n: (m, 0))],
        out_specs=pl.BlockSpec((bm, bn), lambda m, n: (m, n)),
        out_shape=jax.ShapeDtypeStruct((t, n), BF16),
        compiler_params=_cparams("parallel", "parallel"),
        name="q_proj_rope",
    )(hb, ss, w, cos, sin)


def _kv_proj(hb, ss, idx, w, cos, sin):
    _, t, _ = hb.shape
    n = w.shape[2]
    nk = n // 2
    bm = 1024
    return pl.pallas_call(
        _kv_kernel,
        grid=(t // bm, 1),
        in_specs=_stream_specs(hb, idx, bm) + [
            _weight_spec(w, 0, n),
            pl.BlockSpec((bm, LANES), lambda m, n: (m, 0)),
            pl.BlockSpec((bm, LANES), lambda m, n: (m, 0))],
        out_specs=[pl.BlockSpec((bm, nk), lambda m, n: (m, 0))] * 2,
        out_shape=[jax.ShapeDtypeStruct((t, nk), F32)] * 2,
        compiler_params=_cparams("parallel", "arbitrary"),
        name="kv_proj_rope",
    )(hb, ss, w, cos, sin)


def _resid_kernel(a_ref, w_ref, h_ref, g_ref, hn_ref, hb_ref, ss_ref):
    part = None
    for cs in _col_chunks(hn_ref.shape[1]):
        acc = jnp.dot(a_ref[...], w_ref[:, cs], preferred_element_type=F32)
        p = _emit_cols(h_ref[:, cs] + acc, cs, g_ref, hn_ref, hb_ref)
        part = p if part is None else part + p
    _accumulate_ss(ss_ref, part, pl.program_id(1) == 0)


def _ple_kernel(a_ref, ssin_ref, w_ref, p_ref, wup_ref, h_ref, g_ref, hn_ref, hb_ref, ss_ref):
    rstd = _rstd(ssin_ref, a_ref.shape[1])
    pb = p_ref[...].astype(BF16)
    part = None
    for cs in _col_chunks(hn_ref.shape[1]):
        z = jnp.dot(a_ref[...], w_ref[:, cs], preferred_element_type=F32) * rstd
        gate = 1.0 / (1.0 + jnp.exp(-z))
        up = jnp.dot(pb, wup_ref[:, cs], preferred_element_type=F32)
        p = _emit_cols(h_ref[:, cs] + gate * up, cs, g_ref, hn_ref, hb_ref)
        part = p if part is None else part + p
    _accumulate_ss(ss_ref, part, pl.program_id(1) == 0)


def _stream_out(t, d, ng, bm, bn):
    specs = [pl.BlockSpec((bm, bn), lambda m, n: (m, n)),
             pl.BlockSpec((ng, bm, bn), lambda m, n: (0, m, n)),
             pl.BlockSpec((bm, LANES), lambda m, n: (m, 0))]
    shapes = [jax.ShapeDtypeStruct((t, d), F32),
              jax.ShapeDtypeStruct((ng, t, d), BF16),
              jax.ShapeDtypeStruct((t, LANES), F32)]
    return specs, shapes


def _resid_matmul(a, w, layer, h, gains, name):
    t, k = a.shape
    d = w.shape[2]
    ng = gains.shape[0]
    bm, bn = 1024, 512
    out_specs, out_shape = _stream_out(t, d, ng, bm, bn)
    return pl.pallas_call(
        _resid_kernel,
        grid=(t // bm, d // bn),
        in_specs=[pl.BlockSpec((bm, k), lambda m, n: (m, 0)),
                  _weight_spec(w, layer, bn),
                  pl.BlockSpec((bm, bn), lambda m, n: (m, n)),
                  pl.BlockSpec((ng, bn), lambda m, n: (0, n))],
        out_specs=out_specs,
        out_shape=out_shape,
        compiler_params=_cparams("parallel", "arbitrary"),
        name=name,
    )(a, w, h, gains)


def _ple(hb, ss, w_gate, gate_layer, p, w_up, layer, h, gains):
    _, t, _ = hb.shape
    d = w_gate.shape[2]
    ng = gains.shape[0]
    pdim = p.shape[-1]
    bm, bn = 1024, 512
    out_specs, out_shape = _stream_out(t, d, ng, bm, bn)
    return pl.pallas_call(
        _ple_kernel,
        grid=(t // bm, d // bn),
        in_specs=_stream_specs(hb, 0, bm) + [
            _weight_spec(w_gate, gate_layer, bn),
            pl.BlockSpec((None, None, bm, pdim), lambda m, n: (layer, 0, m, 0)),
            _weight_spec(w_up, layer, bn),
            pl.BlockSpec((bm, bn), lambda m, n: (m, n)),
            pl.BlockSpec((ng, bn), lambda m, n: (0, n))],
        out_specs=out_specs,
        out_shape=out_shape,
        compiler_params=_cparams("parallel", "arbitrary"),
        name="ple",
    )(hb, ss, w_gate, p, w_up, h, gains)


def _mlp_kernel(*refs, n_ff, n_cast):
    a_ref, ssin_ref, w1_ref, w2_ref, h_ref, g_ref = refs[:6]
    cast_src = refs[6:6 + n_cast]
    hn_ref, hb_ref, ss_ref = refs[6 + n_cast:9 + n_cast]
    cast_dst = refs[9 + n_cast:9 + 2 * n_cast]
    acc_ref = refs[-1]
    f = pl.program_id(1)
    ec = acc_ref.shape[2]

    @pl.when(f == 0)
    def _():
        acc_ref[...] = jnp.zeros_like(acc_ref)

    @pl.when(f < n_ff)
    def _():
        z = jnp.dot(a_ref[...], w1_ref[...], preferred_element_type=F32)
        z = jnp.maximum(z * _rstd(ssin_ref, a_ref.shape[1]), 0.0)
        act = (z * z).astype(BF16)
        for j in range(acc_ref.shape[0]):
            acc_ref[j] += jnp.dot(act, w2_ref[:, j * ec:(j + 1) * ec], preferred_element_type=F32)
        _cast_blocks(cast_src, cast_dst)

    @pl.when(f >= n_ff)
    def _():
        part = _emit_cols(h_ref[...] + acc_ref[f - n_ff], slice(0, ec), g_ref, hn_ref, hb_ref)
        _accumulate_ss(ss_ref, part, f == n_ff)


def _mlp(hb, ss, w1, w2, layer, h, gains, cast_jobs=()):
    _, t, k = hb.shape
    d_ff = w1.shape[2]
    d = w2.shape[2]
    ng = gains.shape[0]
    bm, bf = 1024, 512
    ec = 512
    n_ff, n_ec = d_ff // bf, d // ec
    last = n_ff - 1

    def epi(m, f):
        return (m, jnp.maximum(f - n_ff, 0))

    cast_in, cast_out, cast_shape = _cast_specs(cast_jobs, (t // bm) * n_ff,
                                                lambda m, f: m * n_ff + jnp.minimum(f, last))
    outs = pl.pallas_call(
        functools.partial(_mlp_kernel, n_ff=n_ff, n_cast=len(cast_jobs)),
        grid=(t // bm, n_ff + n_ec),
        in_specs=[
            pl.BlockSpec((None, bm, k), lambda m, f: (0, m, 0), pipeline_mode=pl.Buffered(1)),
            pl.BlockSpec((bm, LANES), lambda m, f: (m, 0)),
            pl.BlockSpec((None, k, bf), lambda m, f: (layer, 0, jnp.minimum(f, last))),
            pl.BlockSpec((None, bf, d), lambda m, f: (layer, jnp.minimum(f, last), 0)),
            pl.BlockSpec((bm, ec), epi),
            pl.BlockSpec((ng, ec), lambda m, f: (0, jnp.maximum(f - n_ff, 0)))] + cast_in,
        out_specs=[pl.BlockSpec((bm, ec), epi),
                   pl.BlockSpec((ng, bm, ec), lambda m, f: (0, m, jnp.maximum(f - n_ff, 0))),
                   pl.BlockSpec((bm, LANES), lambda m, f: (m, 0))] + cast_out,
        out_shape=[jax.ShapeDtypeStruct((t, d), F32),
                   jax.ShapeDtypeStruct((ng, t, d), BF16),
                   jax.ShapeDtypeStruct((t, LANES), F32)] + cast_shape,
        scratch_shapes=[pltpu.VMEM((n_ec, bm, ec), F32)],
        compiler_params=pltpu.CompilerParams(dimension_semantics=("parallel", "arbitrary"),
                                             vmem_limit_bytes=BIG_VMEM_LIMIT_BYTES),
        name="mlp",
    )(hb, ss, w1, w2, h, gains, *[src for src, _ in cast_jobs])
    return outs[:3], [w[None] for w in outs[3:]]


def _hgrn_kernel(*refs, layer, n_cast):
    q_ref, f_ref, i_ref, g_ref, lbl_ref, ong_ref = refs[:6]
    o_ref = refs[6 + n_cast]
    (st_ref, b_ref, k_ref, qin_ref, qe_ref, ke_ref, kd_ref, dl_ref, gate_ref, oacc_ref,
     upd_ref) = refs[7 + 2 * n_cast:]
    _cast_blocks(refs[6:6 + n_cast], refs[7 + n_cast:7 + 2 * n_cast])
    _hgrn_body(q_ref, f_ref, i_ref, g_ref, lbl_ref, ong_ref, o_ref, st_ref, b_ref, k_ref, qin_ref,
               qe_ref, ke_ref, kd_ref, dl_ref, gate_ref, oacc_ref, upd_ref, layer)


def _hgrn_body(q_ref, f_ref, i_ref, g_ref, lbl_ref, ong_ref, o_ref,
               st_ref, b_ref, k_ref, qin_ref, qe_ref, ke_ref, kd_ref, dl_ref, gate_ref, oacc_ref, upd_ref,
               layer):
    bt = q_ref.shape[0]
    c_len = HG_CHUNK
    half = c_len // 2
    n_chunks = bt // c_len

    @pl.when(pl.program_id(1) == 0)
    def _():
        st_ref[...] = jnp.zeros_like(st_ref)

    lg = lbl_ref[...]
    e = jnp.exp(lg - jnp.max(lg, axis=0, keepdims=True))
    sm = e / jnp.sum(e, axis=0, keepdims=True)
    cs = sm[0:1]
    for l in range(1, layer + 1):
        cs = cs + sm[l:l + 1]
    lb = cs - sm[0:1]

    sig = 1.0 / (1.0 + jnp.exp(-f_ref[...]))
    fgate = lb + (1.0 - lb) * sig
    kk = 1.0 - fgate
    k_ref[...] = kk
    logf = jnp.log(fgate)
    row = lax.broadcasted_iota(jnp.int32, (c_len, HG_HEAD_DIM), 0)
    shifts = [1 << s for s in range(c_len.bit_length() - 1)]
    keep = [row >= shift for shift in shifts]

    for c in range(n_chunks):
        rows = slice(c * c_len, (c + 1) * c_len)
        bc, kc, q = logf[rows], kk[rows], q_ref[rows, :]
        for shift, mask in zip(shifts, keep):
            bc = bc + jnp.where(mask, pltpu.roll(bc, shift, axis=0), 0.0)
        b_ref[rows, :] = bc
        bl = bc[c_len - 1:c_len, :]
        w = bc - bc[half - 1:half, :]
        qin_ref[rows, :] = (q * jnp.exp(bc)).astype(BF16)
        qe_ref[rows, :] = (q * jnp.exp(w)).astype(BF16)
        ke_ref[rows, :] = (kc * jnp.exp(-w)).astype(BF16)
        kd_ref[rows, :] = (kc * jnp.exp(bl - bc)).astype(BF16)
        dl_ref[c:c + 1, :] = jnp.exp(bl)
    gg = g_ref[...]
    gate_ref[...] = ong_ref[...] * gg / (1.0 + jnp.exp(-gg))

    mids = b_ref[pl.ds(half - 1, bt // half, stride=half), :]
    second = (lax.broadcasted_iota(jnp.int32, mids.shape, 0) & 1) == 1
    safe = jnp.max(jnp.where(second, pltpu.roll(mids, 1, axis=0), 0.0) - mids) <= HG_SAFE_DECAY

    tri = (lax.broadcasted_iota(jnp.int32, (c_len, c_len), 1)
           <= lax.broadcasted_iota(jnp.int32, (c_len, c_len), 0))
    rowc = lax.broadcasted_iota(jnp.int32, (c_len, 1), 0)

    def state_increment(sl):
        return jnp.dot(i_ref[sl, :].T.astype(BF16), kd_ref[sl, :], preferred_element_type=F32)

    def finish():
        o = oacc_ref[...]
        on = o * lax.rsqrt(jnp.mean(o * o, axis=-1, keepdims=True) + NORM_EPS)
        o_ref[...] = (on * gate_ref[...]).astype(BF16)

    @pl.when(safe)
    def _():
        sls = [slice(c * c_len, (c + 1) * c_len) for c in range(n_chunks)]
        atts = [lax.dot_general(qe_ref[sl, :], ke_ref[sl, :], DN_LAST, preferred_element_type=F32)
                for sl in sls]
        atts = [jnp.where(tri, att, 0.0).astype(BF16) for att in atts]
        for sl, att in zip(sls, atts):
            oacc_ref[sl, :] = jnp.dot(att, i_ref[sl, :].astype(BF16), preferred_element_type=F32)
        for c, sl in enumerate(sls):
            upd_ref[c] = state_increment(sl)
        st = st_ref[...]
        for c, sl in enumerate(sls):
            oacc_ref[sl, :] += lax.dot_general(qin_ref[sl, :], st.astype(BF16), DN_LAST,
                                               preferred_element_type=F32)
            st = st * dl_ref[c:c + 1, :] + upd_ref[c]
        st_ref[...] = st
        finish()

    @pl.when(jnp.logical_not(safe))
    def _():
        def body(c, carry):
            r0 = pl.multiple_of(c * c_len, c_len)
            sl = pl.ds(r0, c_len)
            st = st_ref[...]
            q, bc = q_ref[sl, :], b_ref[sl, :]

            def key_row(s, acc):
                bs = b_ref[pl.ds(r0 + s, 1), :]
                ks = k_ref[pl.ds(r0 + s, 1), :]
                vs = i_ref[pl.ds(r0 + s, 1), :]
                a = jnp.sum(q * ks * jnp.exp(jnp.minimum(bc - bs, 0.0)), axis=-1, keepdims=True)
                return acc + jnp.where(rowc >= s, a, 0.0) * vs

            o = lax.dot_general(qin_ref[sl, :], st.astype(BF16), DN_LAST, preferred_element_type=F32)
            oacc_ref[sl, :] = o + lax.fori_loop(0, c_len, key_row, jnp.zeros((c_len, HG_HEAD_DIM), F32))
            st_ref[...] = st * dl_ref[pl.ds(c, 1), :] + state_increment(sl)
            return carry
        lax.fori_loop(0, n_chunks, body, 0)
        finish()


def _hgrn_scan(proj, lb_logits, out_norm_g, layer, cast_jobs=()):
    t = proj.shape[0]
    d = proj.shape[1] // 4
    heads = d // HG_HEAD_DIM
    n_a = lb_logits.shape[0]
    bt = 4096

    def col(part):
        return pl.BlockSpec((bt, HG_HEAD_DIM), lambda h, i: (i, part * heads + h))

    def rows(dtype):
        return pltpu.VMEM((bt, HG_HEAD_DIM), dtype)

    n_row = t // bt
    cast_in, cast_out, cast_shape = _cast_specs(cast_jobs, heads * n_row, lambda h, i: h * n_row + i)
    outs = pl.pallas_call(
        functools.partial(_hgrn_kernel, layer=layer, n_cast=len(cast_jobs)),
        grid=(heads, n_row),
        in_specs=[col(0), col(1), col(2), col(3),
                  pl.BlockSpec((n_a, HG_HEAD_DIM), lambda h, i: (0, h)),
                  pl.BlockSpec((None, 1, HG_HEAD_DIM), lambda h, i: (layer, 0, h))] + cast_in,
        out_specs=[pl.BlockSpec((bt, HG_HEAD_DIM), lambda h, i: (i, h))] + cast_out,
        out_shape=[jax.ShapeDtypeStruct((t, d), BF16)] + cast_shape,
        scratch_shapes=[pltpu.VMEM((HG_HEAD_DIM, HG_HEAD_DIM), F32),
                        rows(F32), rows(F32),
                        rows(BF16), rows(BF16), rows(BF16), rows(BF16),
                        pltpu.VMEM((bt // HG_CHUNK, HG_HEAD_DIM), F32),
                        rows(F32), rows(F32),
                        pltpu.VMEM((bt // HG_CHUNK, HG_HEAD_DIM, HG_HEAD_DIM), F32)],
        compiler_params=_cparams("parallel", "arbitrary"),
        name="hgrn_scan",
    )(proj, proj, proj, proj, lb_logits, out_norm_g.reshape(n_a, 1, d), *[src for src, _ in cast_jobs])
    return outs[0], [w[None] for w in outs[1:]]


def _attn_kernel(sink_ref, bias_ref, q_ref, kc_ref, kp_ref, vc_ref, vp_ref, o_ref):
    n, jj = pl.program_id(0), pl.program_id(1)
    heads_per_kv = q_ref.shape[1] // (2 * ATT_HEAD_DIM)
    n_slab = heads_per_kv // 2

    low = lax.broadcasted_iota(jnp.int32, (2 * WINDOW, LANES), 1) < ATT_HEAD_DIM
    ones_lo = jnp.where(low, 1.0, 0.0)
    real_key = lax.broadcasted_iota(jnp.int32, (2 * WINDOW, LANES), 0) > 0
    v_low, v_high = low & real_key, jnp.logical_not(low) & real_key
    sink_col = lax.broadcasted_iota(jnp.int32, (WINDOW, LANES), 1) == 0

    for qb in range(q_ref.shape[0] // WINDOW):
        rows = slice(qb * WINDOW, (qb + 1) * WINDOW)
        if qb == 0:
            k_prev, v_prev = kp_ref[...], vp_ref[...]
            bias = jnp.where(n == 0, bias_ref[0], bias_ref[1])
        else:
            before = slice((qb - 1) * WINDOW, qb * WINDOW)
            k_prev, v_prev = kc_ref[before, :], vc_ref[before, :]
            bias = bias_ref[1]
        k2 = jnp.concatenate([k_prev, kc_ref[rows, :]], axis=0)
        v2 = jnp.concatenate([v_prev, vc_ref[rows, :]], axis=0)
        k2r, v2r = pltpu.roll(k2, ATT_HEAD_DIM, axis=1), pltpu.roll(v2, ATT_HEAD_DIM, axis=1)

        scores, v_exts = [], []
        for kvh in range(2):
            k_lo, k_hi = (k2, k2r) if kvh == 0 else (k2r, k2)
            v_lo, v_hi = (v2, v2r) if kvh == 0 else (v2r, v2)
            k_half = (jnp.where(low, k_lo, 0.0).astype(BF16), jnp.where(low, 0.0, k_hi).astype(BF16))
            v_exts.append((jnp.concatenate([jnp.where(v_low, v_lo, 0.0), ones_lo], axis=1).astype(BF16),
                           jnp.concatenate([jnp.where(v_high, v_hi, 0.0), 1.0 - ones_lo], axis=1).astype(BF16)))
            col0 = kvh * n_slab * LANES
            qs = jnp.concatenate([q_ref[rows, col0 + p * LANES:col0 + (p + 1) * LANES] for p in range(n_slab)],
                                 axis=0)
            scores.append([lax.dot_general(qs, k_half[parity], DN_LAST, preferred_element_type=F32)
                           for parity in range(2)])

        probs = []
        for kvh in range(2):
            pe = []
            for parity in range(2):
                head0 = (2 * jj + kvh) * heads_per_kv + parity
                slabs = []
                for p in range(n_slab):
                    sp = scores[kvh][parity][p * WINDOW:(p + 1) * WINDOW, :] + bias
                    sp = jnp.concatenate([jnp.where(sink_col, sink_ref[head0 + 2 * p], sp[:, :LANES]),
                                          sp[:, LANES:]], axis=1)
                    slabs.append(jnp.exp(sp - jnp.max(sp, axis=-1, keepdims=True)).astype(BF16))
                pe.append(jnp.concatenate(slabs, axis=0))
            probs.append(pe)

        for kvh in range(2):
            res = [jnp.dot(probs[kvh][parity], v_exts[kvh][parity], preferred_element_type=F32)
                   for parity in range(2)]
            out = (res[0][:, :LANES] + res[1][:, :LANES]) / (res[0][:, LANES:] + res[1][:, LANES:])
            col0 = kvh * n_slab * LANES
            for p in range(n_slab):
                o_ref[rows, col0 + p * LANES:col0 + (p + 1) * LANES] = (
                    out[p * WINDOW:(p + 1) * WINDOW, :].astype(BF16))


def _attention(q, k, v, sinks):
    t, dq = q.shape
    qb = 4
    rows = qb * WINDOW
    gw = 2 * dq // ATT_KV_HEADS
    a_idx = jnp.arange(WINDOW)[:, None]
    c_idx = jnp.arange(2 * WINDOW)[None, :]
    band = (c_idx >= a_idx + 1) & (c_idx <= a_idx + WINDOW)
    bias = jnp.stack([jnp.where(band & (c_idx >= WINDOW), 0.0, -jnp.inf),
                      jnp.where(band, 0.0, -jnp.inf)]).astype(F32)

    def cur(n, j):
        return (n, j)

    def prev(n, j):
        return (jnp.maximum(qb * n - 1, 0), j)

    return pl.pallas_call(
        _attn_kernel,
        grid=(t // rows, ATT_KV_HEADS // 2),
        in_specs=[pl.BlockSpec(memory_space=pltpu.SMEM),
                  pl.BlockSpec((2, WINDOW, 2 * WINDOW), lambda n, j: (0, 0, 0)),
                  pl.BlockSpec((rows, gw), lambda n, j: (n, j)),
                  pl.BlockSpec((rows, LANES), cur),
                  pl.BlockSpec((WINDOW, LANES), prev),
                  pl.BlockSpec((rows, LANES), cur),
                  pl.BlockSpec((WINDOW, LANES), prev)],
        out_specs=pl.BlockSpec((rows, gw), lambda n, j: (n, j)),
        out_shape=jax.ShapeDtypeStruct((t, dq), BF16),
        compiler_params=_cparams("parallel", "parallel"),
        name="swa_attention",
    )(sinks, bias, q, k, k, v, v)


def kernel(x, p, positions, mixer_norm_g, hgrn_w_in, hgrn_w_out, hgrn_lb_logits, hgrn_out_norm_g,
           kv_norm_g, w_kv, attn_w_q, attn_w_o, attn_sinks, mlp_norm_g, mlp_w1, mlp_w2,
           ple_norm_g, ple_w_gate, ple_w_up, final_norm_g):
    batch, t, d = x.shape
    depth = mlp_w1.shape[0]
    n_a = hgrn_w_in.shape[0]
    assert batch == 1

    w_in, w_kvb, w_up = hgrn_w_in[0:1].astype(BF16), w_kv.astype(BF16)[None], ple_w_up.astype(BF16)
    mixer_w = [hgrn_w_out[0:1].astype(BF16)] if n_a > 0 else [attn_w_q[0:1].astype(BF16),
                                                              attn_w_o[0:1].astype(BF16)]
    if n_a == 0:
        w1, w2 = mlp_w1[0:1].astype(BF16), mlp_w2[0:1].astype(BF16)

    cos, sin = _rope_tables(positions)
    h = x.reshape(t, d)
    hb, ss = _prep_stream(h, mixer_norm_g[0])
    k_sh = v_sh = None

    for layer in range(depth):
        mlp_gain = mlp_norm_g[layer][None, :]
        if layer < n_a:
            jobs = [(mlp_w1, 0), (mlp_w2, 0)] if layer == 0 else []
            proj, cast = _proj(hb, ss, 0, w_in, 0, F32, "hgrn_in_proj", jobs)
            if layer == 0:
                w1, w2 = cast
            jobs = [(hgrn_w_in, layer + 1)] if layer + 1 < n_a else []
            mixed, cast = _hgrn_scan(proj, hgrn_lb_logits, hgrn_out_norm_g, layer, jobs)
            if jobs:
                w_in = cast[0]
            h, hb, ss = _resid_matmul(mixed, mixer_w[0], 0, h, mlp_gain, "hgrn_out_proj")
        else:
            j = layer - n_a
            q = _proj_rope(hb, ss, 0, mixer_w[0], 0, cos, sin, ATT_HEAD_DIM ** -0.5)
            mixed = _attention(q, k_sh, v_sh, attn_sinks[j])
            h, hb, ss = _resid_matmul(mixed, mixer_w[1], 0, h, mlp_gain, "attn_out_proj")

        jobs = [(ple_w_gate, layer)]
        if layer + 1 < depth:
            jobs += [(mlp_w1, layer + 1), (mlp_w2, layer + 1)]
            jobs += ([(hgrn_w_out, layer + 1)] if layer + 1 < n_a
                     else [(attn_w_q, layer + 1 - n_a), (attn_w_o, layer + 1 - n_a)])
        (h, hb, ss), cast = _mlp(hb, ss, w1, w2, 0, h, ple_norm_g[layer][None, :], jobs)
        w_gate = cast[0]
        if layer + 1 < depth:
            w1, w2, mixer_w = cast[1], cast[2], cast[3:]

        if layer + 1 < depth:
            next_gains = [mixer_norm_g[layer + 1]]
            if layer == n_a - 1:
                next_gains.append(kv_norm_g)
        else:
            next_gains = [final_norm_g]
        h, hb, ss = _ple(hb, ss, w_gate, 0, p, w_up, layer, h, jnp.stack(next_gains))

        if layer == n_a - 1:
            k_sh, v_sh = _kv_proj(hb, ss, 1, w_kvb, cos, sin)

    return _final_norm(h, ss, final_norm_g).reshape(batch, t, d)
```

```python
import functools

import jax
import jax.numpy as jnp
from jax import lax
from jax.experimental import pallas as pl
from jax.experimental.pallas import tpu as pltpu

F32 = jnp.float32
BF16 = jnp.bfloat16

LANES = 128
MXU_COLS = 256
VMEM_LIMIT_BYTES = 56 * 1024 * 1024
BIG_VMEM_LIMIT_BYTES = 63 * 1024 * 1024

HG_HEAD_DIM = 128
HG_CHUNK = 64
HG_SAFE_DECAY = 60.0
ATT_HEAD_DIM = 64
ATT_KV_HEADS = 8
WINDOW = 128
ROPE_THETA = 10000.0
NORM_EPS = 1e-6

DN_LAST = (((1,), (1,)), ((), ()))


def _cparams(*sem):
    return pltpu.CompilerParams(dimension_semantics=sem, vmem_limit_bytes=VMEM_LIMIT_BYTES)


def _rstd(ss_ref, d_model):
    return lax.rsqrt(jnp.sum(ss_ref[...], axis=-1, keepdims=True) * (1.0 / d_model) + NORM_EPS)


def _lane_partial_sumsq(x):
    sq = x * x
    acc = sq[:, 0:LANES]
    for c in range(1, x.shape[1] // LANES):
        acc = acc + sq[:, c * LANES:(c + 1) * LANES]
    return acc


def _col_chunks(width):
    return [slice(c, c + MXU_COLS) for c in range(0, width, MXU_COLS)]


def _emit_cols(h_new, cs, g_ref, hn_ref, hb_ref):
    hn_ref[:, cs] = h_new
    for i in range(hb_ref.shape[0]):
        hb_ref[i, :, cs] = (h_new * g_ref[i:i + 1, cs]).astype(BF16)
    return _lane_partial_sumsq(h_new)


def _accumulate_ss(ss_ref, part, first):
    @pl.when(first)
    def _():
        ss_ref[...] = jnp.zeros_like(ss_ref)
    ss_ref[...] += part


def _cast_specs(jobs, n_steps, step_of):
    in_specs, out_specs, out_shape = [], [], []
    for src, layer in jobs:
        _, k, n = src.shape
        rows = k // n_steps
        assert rows * n_steps == k and rows % 16 == 0
        in_specs.append(pl.BlockSpec((None, rows, n), lambda *g, layer=layer: (layer, step_of(*g), 0)))
        out_specs.append(pl.BlockSpec((rows, n), lambda *g: (step_of(*g), 0)))
        out_shape.append(jax.ShapeDtypeStruct((k, n), BF16))
    return in_specs, out_specs, out_shape


def _cast_blocks(src_refs, dst_refs):
    for src, dst in zip(src_refs, dst_refs):
        dst[...] = src[...].astype(BF16)


def _swap_half_heads(x):
    lane = lax.broadcasted_iota(jnp.int32, x.shape, 1)
    first_half = (lane & (ATT_HEAD_DIM - 1)) < (ATT_HEAD_DIM // 2)
    return jnp.where(first_half, pltpu.roll(x, LANES - ATT_HEAD_DIM // 2, axis=1),
                     pltpu.roll(x, ATT_HEAD_DIM // 2, axis=1))


def _rope_store(x, cos, sin_signed, o_ref, col0):
    for c in range(x.shape[1] // LANES):
        xc = x[:, c * LANES:(c + 1) * LANES]
        piece = xc * cos + _swap_half_heads(xc) * sin_signed
        o_ref[:, col0 + c * LANES:col0 + (c + 1) * LANES] = piece.astype(o_ref.dtype)


def _rope_table_kernel(pos_ref, invf_ref, sign_ref, cos_ref, sin_ref):
    ang = pos_ref[...].astype(F32) * invf_ref[...]
    cos_ref[...] = jnp.cos(ang)
    sin_ref[...] = jnp.sin(ang) * sign_ref[...]


def _rope_tables(positions):
    t = positions.shape[-1]
    half = ATT_HEAD_DIM // 2
    inv_freq = ROPE_THETA ** (-jnp.arange(half, dtype=F32) / half)
    invf = jnp.tile(inv_freq, LANES // half)[None, :]
    sign = jnp.tile(jnp.concatenate([-jnp.ones((half,), F32), jnp.ones((half,), F32)]),
                    LANES // ATT_HEAD_DIM)[None, :]
    bt = 1024
    return pl.pallas_call(
        _rope_table_kernel,
        grid=(t // bt,),
        in_specs=[pl.BlockSpec((bt, 1), lambda i: (i, 0)),
                  pl.BlockSpec((1, LANES), lambda i: (0, 0)),
                  pl.BlockSpec((1, LANES), lambda i: (0, 0))],
        out_specs=[pl.BlockSpec((bt, LANES), lambda i: (i, 0))] * 2,
        out_shape=[jax.ShapeDtypeStruct((t, LANES), F32)] * 2,
        compiler_params=_cparams("parallel"),
        name="rope_tables",
    )(positions.reshape(t, 1), invf, sign)


def _prep_kernel(x_ref, g_ref, hb_ref, ss_ref):
    x = x_ref[...]
    hb_ref[0] = (x * g_ref[...]).astype(BF16)
    ss_ref[...] = _lane_partial_sumsq(x)


def _prep_stream(x, g):
    t, d = x.shape
    bt = 256
    return pl.pallas_call(
        _prep_kernel,
        grid=(t // bt,),
        in_specs=[pl.BlockSpec((bt, d), lambda i: (i, 0)),
                  pl.BlockSpec((1, d), lambda i: (0, 0))],
        out_specs=[pl.BlockSpec((1, bt, d), lambda i: (0, i, 0)),
                   pl.BlockSpec((bt, LANES), lambda i: (i, 0))],
        out_shape=[jax.ShapeDtypeStruct((1, t, d), BF16),
                   jax.ShapeDtypeStruct((t, LANES), F32)],
        compiler_params=_cparams("parallel"),
        name="prep_stream",
    )(x, g.reshape(1, d))


def _final_norm_kernel(h_ref, ss_ref, g_ref, o_ref):
    o_ref[...] = h_ref[...] * _rstd(ss_ref, h_ref.shape[1]) * g_ref[...]


def _final_norm(h, ss, g):
    t, d = h.shape
    bt = 256
    return pl.pallas_call(
        _final_norm_kernel,
        grid=(t // bt,),
        in_specs=[pl.BlockSpec((bt, d), lambda i: (i, 0)),
                  pl.BlockSpec((bt, LANES), lambda i: (i, 0)),
                  pl.BlockSpec((1, d), lambda i: (0, 0))],
        out_specs=pl.BlockSpec((bt, d), lambda i: (i, 0)),
        out_shape=jax.ShapeDtypeStruct((t, d), F32),
        compiler_params=_cparams("parallel"),
        name="final_norm",
    )(h, ss, g.reshape(1, d))


def _proj_kernel(*refs, n_cast):
    a_ref, ss_ref, w_ref = refs[:3]
    o_ref = refs[3 + n_cast]
    acc = jnp.dot(a_ref[...], w_ref[...], preferred_element_type=F32)
    o_ref[...] = (acc * _rstd(ss_ref, a_ref.shape[1])).astype(o_ref.dtype)
    _cast_blocks(refs[3:3 + n_cast], refs[4 + n_cast:])


def _proj_rope_kernel(a_ref, ss_ref, w_ref, cos_ref, sin_ref, o_ref, *, scale):
    row_scale = _rstd(ss_ref, a_ref.shape[1]) * scale
    cos, sin = cos_ref[...], sin_ref[...]
    for cs in _col_chunks(o_ref.shape[1]):
        acc = jnp.dot(a_ref[...], w_ref[:, cs], preferred_element_type=F32)
        _rope_store(acc * row_scale, cos, sin, o_ref, cs.start)


def _kv_kernel(a_ref, ss_ref, w_ref, cos_ref, sin_ref, k_ref, v_ref):
    rstd = _rstd(ss_ref, a_ref.shape[1])
    nk = k_ref.shape[1]
    cos, sin = cos_ref[...], sin_ref[...]
    for cs in _col_chunks(nk):
        acc = jnp.dot(a_ref[...], w_ref[:, cs], preferred_element_type=F32)
        _rope_store(acc * rstd, cos, sin, k_ref, cs.start)
    for cs in _col_chunks(nk):
        acc = jnp.dot(a_ref[...], w_ref[:, nk + cs.start:nk + cs.stop], preferred_element_type=F32)
        v_ref[:, cs] = acc * rstd


def _stream_specs(hb, idx, bm):
    k = hb.shape[2]
    return [pl.BlockSpec((None, bm, k), lambda m, n: (idx, m, 0)),
            pl.BlockSpec((bm, LANES), lambda m, n: (m, 0))]


def _weight_spec(w, layer, bn):
    return pl.BlockSpec((None, w.shape[1], bn), lambda m, n: (layer, 0, n))


def _proj(hb, ss, idx, w, layer, out_dtype, name, cast_jobs=()):
    _, t, _ = hb.shape
    n = w.shape[2]
    bm, bn = 1024, 1024
    n_col = n // bn
    cast_in, cast_out, cast_shape = _cast_specs(cast_jobs, (t // bm) * n_col, lambda m, n: m * n_col + n)
    outs = pl.pallas_call(
        functools.partial(_proj_kernel, n_cast=len(cast_jobs)),
        grid=(t // bm, n_col),
        in_specs=_stream_specs(hb, idx, bm) + [_weight_spec(w, layer, bn)] + cast_in,
        out_specs=[pl.BlockSpec((bm, bn), lambda m, n: (m, n))] + cast_out,
        out_shape=[jax.ShapeDtypeStruct((t, n), out_dtype)] + cast_shape,
        compiler_params=pltpu.CompilerParams(dimension_semantics=("parallel", "parallel"),
                                             vmem_limit_bytes=BIG_VMEM_LIMIT_BYTES),
        name=name,
    )(hb, ss, w, *[src for src, _ in cast_jobs])
    return outs[0], [w[None] for w in outs[1:]]


def _proj_rope(hb, ss, idx, w, layer, cos, sin, scale):
    _, t, _ = hb.shape
    n = w.shape[2]
    bm, bn = 1024, 1024
    return pl.pallas_call(
        functools.partial(_proj_rope_kernel, scale=scale),
        grid=(t // bm, n // bn),
        in_specs=_stream_specs(hb, idx, bm) + [
            _weight_spec(w, layer, bn),
            pl.BlockSpec((bm, LANES), lambda m, n: (m, 0)),
            pl.BlockSpec((bm, LANES), lambda m, n: (m, 0))],
        out_specs=pl.BlockSpec((bm, bn), lambda m, n: (m, n)),
        out_shape=jax.ShapeDtypeStruct((t, n), BF16),
        compiler_params=_cparams("parallel", "parallel"),
        name="q_proj_rope",
    )(hb, ss, w, cos, sin)


def _kv_proj(hb, ss, idx, w, cos, sin):
    _, t, _ = hb.shape
    n = w.shape[2]
    nk = n // 2
    bm = 1024
    return pl.pallas_call(
        _kv_kernel,
        grid=(t // bm, 1),
        in_specs=_stream_specs(hb, idx, bm) + [
            _weight_spec(w, 0, n),
            pl.BlockSpec((bm, LANES), lambda m, n: (m, 0)),
            pl.BlockSpec((bm, LANES), lambda m, n: (m, 0))],
        out_specs=[pl.BlockSpec((bm, nk), lambda m, n: (m, 0))] * 2,
        out_shape=[jax.ShapeDtypeStruct((t, nk), F32)] * 2,
        compiler_params=_cparams("parallel", "arbitrary"),
        name="kv_proj_rope",
    )(hb, ss, w, cos, sin)


def _resid_kernel(a_ref, w_ref, h_ref, g_ref, hn_ref, hb_ref, ss_ref):
    part = None
    for cs in _col_chunks(hn_ref.shape[1]):
        acc = jnp.dot(a_ref[...], w_ref[:, cs], preferred_element_type=F32)
        p = _emit_cols(h_ref[:, cs] + acc, cs, g_ref, hn_ref, hb_ref)
        part = p if part is None else part + p
    _accumulate_ss(ss_ref, part, pl.program_id(1) == 0)


def _ple_kernel(a_ref, ssin_ref, w_ref, p_ref, wup_ref, h_ref, g_ref, hn_ref, hb_ref, ss_ref):
    rstd = _rstd(ssin_ref, a_ref.shape[1])
    pb = p_ref[...].astype(BF16)
    part = None
    for cs in _col_chunks(hn_ref.shape[1]):
        z = jnp.dot(a_ref[...], w_ref[:, cs], preferred_element_type=F32) * rstd
        gate = 1.0 / (1.0 + jnp.exp(-z))
        up = jnp.dot(pb, wup_ref[:, cs], preferred_element_type=F32)
        p = _emit_cols(h_ref[:, cs] + gate * up, cs, g_ref, hn_ref, hb_ref)
        part = p if part is None else part + p
    _accumulate_ss(ss_ref, part, pl.program_id(1) == 0)


def _stream_out(t, d, ng, bm, bn):
    specs = [pl.BlockSpec((bm, bn), lambda m, n: (m, n)),
             pl.BlockSpec((ng, bm, bn), lambda m, n: (0, m, n)),
             pl.BlockSpec((bm, LANES), lambda m, n: (m, 0))]
    shapes = [jax.ShapeDtypeStruct((t, d), F32),
              jax.ShapeDtypeStruct((ng, t, d), BF16),
              jax.ShapeDtypeStruct((t, LANES), F32)]
    return specs, shapes


def _resid_matmul(a, w, layer, h, gains, name):
    t, k = a.shape
    d = w.shape[2]
    ng = gains.shape[0]
    bm, bn = 1024, 1024
    out_specs, out_shape = _stream_out(t, d, ng, bm, bn)
    return pl.pallas_call(
        _resid_kernel,
        grid=(t // bm, d // bn),
        in_specs=[pl.BlockSpec((bm, k), lambda m, n: (m, 0)),
                  _weight_spec(w, layer, bn),
                  pl.BlockSpec((bm, bn), lambda m, n: (m, n)),
                  pl.BlockSpec((ng, bn), lambda m, n: (0, n))],
        out_specs=out_specs,
        out_shape=out_shape,
        compiler_params=pltpu.CompilerParams(dimension_semantics=("parallel", "arbitrary"),
                                             vmem_limit_bytes=BIG_VMEM_LIMIT_BYTES),
        name=name,
    )(a, w, h, gains)


def _ple(hb, ss, w_gate, gate_layer, p, w_up, layer, h, gains):
    _, t, _ = hb.shape
    d = w_gate.shape[2]
    ng = gains.shape[0]
    pdim = p.shape[-1]
    bm, bn = 1024, 512
    out_specs, out_shape = _stream_out(t, d, ng, bm, bn)
    return pl.pallas_call(
        _ple_kernel,
        grid=(t // bm, d // bn),
        in_specs=_stream_specs(hb, 0, bm) + [
            _weight_spec(w_gate, gate_layer, bn),
            pl.BlockSpec((None, None, bm, pdim), lambda m, n: (layer, 0, m, 0)),
            _weight_spec(w_up, layer, bn),
            pl.BlockSpec((bm, bn), lambda m, n: (m, n)),
            pl.BlockSpec((ng, bn), lambda m, n: (0, n))],
        out_specs=out_specs,
        out_shape=out_shape,
        compiler_params=_cparams("parallel", "arbitrary"),
        name="ple",
    )(hb, ss, w_gate, p, w_up, h, gains)


def _mlp_kernel(*refs, n_ff, n_cast):
    a_ref, ssin_ref, w1_ref, w2_ref, h_ref, g_ref = refs[:6]
    cast_src = refs[6:6 + n_cast]
    hn_ref, hb_ref, ss_ref = refs[6 + n_cast:9 + n_cast]
    cast_dst = refs[9 + n_cast:9 + 2 * n_cast]
    acc_ref = refs[-1]
    f = pl.program_id(1)
    ec = acc_ref.shape[2]

    @pl.when(f == 0)
    def _():
        acc_ref[...] = jnp.zeros_like(acc_ref)

    @pl.when(f < n_ff)
    def _():
        z = jnp.dot(a_ref[...], w1_ref[...], preferred_element_type=F32)
        z = jnp.maximum(z * _rstd(ssin_ref, a_ref.shape[1]), 0.0)
        act = (z * z).astype(BF16)
        for j in range(acc_ref.shape[0]):
            acc_ref[j] += jnp.dot(act, w2_ref[:, j * ec:(j + 1) * ec], preferred_element_type=F32)
        _cast_blocks(cast_src, cast_dst)

    @pl.when(f >= n_ff)
    def _():
        part = _emit_cols(h_ref[...] + acc_ref[f - n_ff], slice(0, ec), g_ref, hn_ref, hb_ref)
        _accumulate_ss(ss_ref, part, f == n_ff)


def _mlp(hb, ss, w1, w2, layer, h, gains, cast_jobs=()):
    _, t, k = hb.shape
    d_ff = w1.shape[2]
    d = w2.shape[2]
    ng = gains.shape[0]
    bm, bf = 1024, 512
    ec = 512
    n_ff, n_ec = d_ff // bf, d // ec
    last = n_ff - 1

    def epi(m, f):
        return (m, jnp.maximum(f - n_ff, 0))

    cast_in, cast_out, cast_shape = _cast_specs(cast_jobs, (t // bm) * n_ff,
                                                lambda m, f: m * n_ff + jnp.minimum(f, last))
    outs = pl.pallas_call(
        functools.partial(_mlp_kernel, n_ff=n_ff, n_cast=len(cast_jobs)),
        grid=(t // bm, n_ff + n_ec),
        in_specs=[
            pl.BlockSpec((None, bm, k), lambda m, f: (0, m, 0), pipeline_mode=pl.Buffered(1)),
            pl.BlockSpec((bm, LANES), lambda m, f: (m, 0)),
            pl.BlockSpec((None, k, bf), lambda m, f: (layer, 0, jnp.minimum(f, last))),
            pl.BlockSpec((None, bf, d), lambda m, f: (layer, jnp.minimum(f, last), 0)),
            pl.BlockSpec((bm, ec), epi),
            pl.BlockSpec((ng, ec), lambda m, f: (0, jnp.maximum(f - n_ff, 0)))] + cast_in,
        out_specs=[pl.BlockSpec((bm, ec), epi),
                   pl.BlockSpec((ng, bm, ec), lambda m, f: (0, m, jnp.maximum(f - n_ff, 0))),
                   pl.BlockSpec((bm, LANES), lambda m, f: (m, 0))] + cast_out,
        out_shape=[jax.ShapeDtypeStruct((t, d), F32),
                   jax.ShapeDtypeStruct((ng, t, d), BF16),
                   jax.ShapeDtypeStruct((t, LANES), F32)] + cast_shape,
        scratch_shapes=[pltpu.VMEM((n_ec, bm, ec), F32)],
        compiler_params=pltpu.CompilerParams(dimension_semantics=("parallel", "arbitrary"),
                                             vmem_limit_bytes=BIG_VMEM_LIMIT_BYTES),
        name="mlp",
    )(hb, ss, w1, w2, h, gains, *[src for src, _ in cast_jobs])
    return outs[:3], [w[None] for w in outs[3:]]


def _hgrn_kernel(*refs, layer, n_cast):
    q_ref, f_ref, i_ref, g_ref, lbl_ref, ong_ref = refs[:6]
    o_ref = refs[6 + n_cast]
    (st_ref, b_ref, k_ref, qin_ref, qe_ref, ke_ref, kd_ref, dl_ref, gate_ref, oacc_ref,
     upd_ref) = refs[7 + 2 * n_cast:]
    _cast_blocks(refs[6:6 + n_cast], refs[7 + n_cast:7 + 2 * n_cast])
    _hgrn_body(q_ref, f_ref, i_ref, g_ref, lbl_ref, ong_ref, o_ref, st_ref, b_ref, k_ref, qin_ref,
               qe_ref, ke_ref, kd_ref, dl_ref, gate_ref, oacc_ref, upd_ref, layer)


def _hgrn_body(q_ref, f_ref, i_ref, g_ref, lbl_ref, ong_ref, o_ref,
               st_ref, b_ref, k_ref, qin_ref, qe_ref, ke_ref, kd_ref, dl_ref, gate_ref, oacc_ref, upd_ref,
               layer):
    bt = q_ref.shape[0]
    c_len = HG_CHUNK
    half = c_len // 2
    n_chunks = bt // c_len

    @pl.when(pl.program_id(1) == 0)
    def _():
        st_ref[...] = jnp.zeros_like(st_ref)

    lg = lbl_ref[...]
    e = jnp.exp(lg - jnp.max(lg, axis=0, keepdims=True))
    sm = e / jnp.sum(e, axis=0, keepdims=True)
    cs = sm[0:1]
    for l in range(1, layer + 1):
        cs = cs + sm[l:l + 1]
    lb = cs - sm[0:1]

    sig = 1.0 / (1.0 + jnp.exp(-f_ref[...]))
    fgate = lb + (1.0 - lb) * sig
    kk = 1.0 - fgate
    k_ref[...] = kk
    logf = jnp.log(fgate)
    row = lax.broadcasted_iota(jnp.int32, (c_len, HG_HEAD_DIM), 0)
    shifts = [1 << s for s in range(c_len.bit_length() - 1)]
    keep = [row >= shift for shift in shifts]

    for c in range(n_chunks):
        rows = slice(c * c_len, (c + 1) * c_len)
        bc, kc, q = logf[rows], kk[rows], q_ref[rows, :]
        for shift, mask in zip(shifts, keep):
            bc = bc + jnp.where(mask, pltpu.roll(bc, shift, axis=0), 0.0)
        b_ref[rows, :] = bc
        bl = bc[c_len - 1:c_len, :]
        w = bc - bc[half - 1:half, :]
        qin_ref[rows, :] = (q * jnp.exp(bc)).astype(BF16)
        qe_ref[rows, :] = (q * jnp.exp(w)).astype(BF16)
        ke_ref[rows, :] = (kc * jnp.exp(-w)).astype(BF16)
        kd_ref[rows, :] = (kc * jnp.exp(bl - bc)).astype(BF16)
        dl_ref[c:c + 1, :] = jnp.exp(bl)
    gg = g_ref[...]
    gate_ref[...] = ong_ref[...] * gg / (1.0 + jnp.exp(-gg))

    mids = b_ref[pl.ds(half - 1, bt // half, stride=half), :]
    second = (lax.broadcasted_iota(jnp.int32, mids.shape, 0) & 1) == 1
    safe = jnp.max(jnp.where(second, pltpu.roll(mids, 1, axis=0), 0.0) - mids) <= HG_SAFE_DECAY

    tri = (lax.broadcasted_iota(jnp.int32, (c_len, c_len), 1)
           <= lax.broadcasted_iota(jnp.int32, (c_len, c_len), 0))
    rowc = lax.broadcasted_iota(jnp.int32, (c_len, 1), 0)

    def state_increment(sl):
        return jnp.dot(i_ref[sl, :].T.astype(BF16), kd_ref[sl, :], preferred_element_type=F32)

    def finish():
        o = oacc_ref[...]
        on = o * lax.rsqrt(jnp.mean(o * o, axis=-1, keepdims=True) + NORM_EPS)
        o_ref[...] = (on * gate_ref[...]).astype(BF16)

    @pl.when(safe)
    def _():
        sls = [slice(c * c_len, (c + 1) * c_len) for c in range(n_chunks)]
        atts = [lax.dot_general(qe_ref[sl, :], ke_ref[sl, :], DN_LAST, preferred_element_type=F32)
                for sl in sls]
        atts = [jnp.where(tri, att, 0.0).astype(BF16) for att in atts]
        for sl, att in zip(sls, atts):
            oacc_ref[sl, :] = jnp.dot(att, i_ref[sl, :].astype(BF16), preferred_element_type=F32)
        for c, sl in enumerate(sls):
            upd_ref[c] = state_increment(sl)
        st = st_ref[...]
        for c, sl in enumerate(sls):
            oacc_ref[sl, :] += lax.dot_general(qin_ref[sl, :], st.astype(BF16), DN_LAST,
                                               preferred_element_type=F32)
            st = st * dl_ref[c:c + 1, :] + upd_ref[c]
        st_ref[...] = st
        finish()

    @pl.when(jnp.logical_not(safe))
    def _():
        def body(c, carry):
            r0 = pl.multiple_of(c * c_len, c_len)
            sl = pl.ds(r0, c_len)
            st = st_ref[...]
            q, bc = q_ref[sl, :], b_ref[sl, :]

            def key_row(s, acc):
                bs = b_ref[pl.ds(r0 + s, 1), :]
                ks = k_ref[pl.ds(r0 + s, 1), :]
                vs = i_ref[pl.ds(r0 + s, 1), :]
                a = jnp.sum(q * ks * jnp.exp(jnp.minimum(bc - bs, 0.0)), axis=-1, keepdims=True)
                return acc + jnp.where(rowc >= s, a, 0.0) * vs

            o = lax.dot_general(qin_ref[sl, :], st.astype(BF16), DN_LAST, preferred_element_type=F32)
            oacc_ref[sl, :] = o + lax.fori_loop(0, c_len, key_row, jnp.zeros((c_len, HG_HEAD_DIM), F32))
            st_ref[...] = st * dl_ref[pl.ds(c, 1), :] + state_increment(sl)
            return carry
        lax.fori_loop(0, n_chunks, body, 0)
        finish()


def _hgrn_scan(proj, lb_logits, out_norm_g, layer, cast_jobs=()):
    t = proj.shape[0]
    d = proj.shape[1] // 4
    heads = d // HG_HEAD_DIM
    n_a = lb_logits.shape[0]
    bt = 4096

    def col(part):
        return pl.BlockSpec((bt, HG_HEAD_DIM), lambda h, i: (i, part * heads + h))

    def rows(dtype):
        return pltpu.VMEM((bt, HG_HEAD_DIM), dtype)

    n_row = t // bt
    cast_in, cast_out, cast_shape = _cast_specs(cast_jobs, heads * n_row, lambda h, i: h * n_row + i)
    outs = pl.pallas_call(
        functools.partial(_hgrn_kernel, layer=layer, n_cast=len(cast_jobs)),
        grid=(heads, n_row),
        in_specs=[col(0), col(1), col(2), col(3),
                  pl.BlockSpec((n_a, HG_HEAD_DIM), lambda h, i: (0, h)),
                  pl.BlockSpec((None, 1, HG_HEAD_DIM), lambda h, i: (layer, 0, h))] + cast_in,
        out_specs=[pl.BlockSpec((bt, HG_HEAD_DIM), lambda h, i: (i, h))] + cast_out,
        out_shape=[jax.ShapeDtypeStruct((t, d), BF16)] + cast_shape,
        scratch_shapes=[pltpu.VMEM((HG_HEAD_DIM, HG_HEAD_DIM), F32),
                        rows(F32), rows(F32),
                        rows(BF16), rows(BF16), rows(BF16), rows(BF16),
                        pltpu.VMEM((bt // HG_CHUNK, HG_HEAD_DIM), F32),
                        rows(F32), rows(F32),
                        pltpu.VMEM((bt // HG_CHUNK, HG_HEAD_DIM, HG_HEAD_DIM), F32)],
        compiler_params=_cparams("parallel", "arbitrary"),
        name="hgrn_scan",
    )(proj, proj, proj, proj, lb_logits, out_norm_g.reshape(n_a, 1, d), *[src for src, _ in cast_jobs])
    return outs[0], [w[None] for w in outs[1:]]


def _attn_kernel(sink_ref, bias_ref, q_ref, kc_ref, kp_ref, vc_ref, vp_ref, o_ref):
    n, jj = pl.program_id(0), pl.program_id(1)
    heads_per_kv = q_ref.shape[1] // (2 * ATT_HEAD_DIM)
    n_slab = heads_per_kv // 2

    low = lax.broadcasted_iota(jnp.int32, (2 * WINDOW, LANES), 1) < ATT_HEAD_DIM
    ones_lo = jnp.where(low, 1.0, 0.0)
    real_key = lax.broadcasted_iota(jnp.int32, (2 * WINDOW, LANES), 0) > 0
    v_low, v_high = low & real_key, jnp.logical_not(low) & real_key
    sink_col = lax.broadcasted_iota(jnp.int32, (WINDOW, LANES), 1) == 0

    for qb in range(q_ref.shape[0] // WINDOW):
        rows = slice(qb * WINDOW, (qb + 1) * WINDOW)
        if qb == 0:
            k_prev, v_prev = kp_ref[...], vp_ref[...]
            bias = jnp.where(n == 0, bias_ref[0], bias_ref[1])
        else:
            before = slice((qb - 1) * WINDOW, qb * WINDOW)
            k_prev, v_prev = kc_ref[before, :], vc_ref[before, :]
            bias = bias_ref[1]
        k2 = jnp.concatenate([k_prev, kc_ref[rows, :]], axis=0)
        v2 = jnp.concatenate([v_prev, vc_ref[rows, :]], axis=0)
        k2r, v2r = pltpu.roll(k2, ATT_HEAD_DIM, axis=1), pltpu.roll(v2, ATT_HEAD_DIM, axis=1)

        scores, v_exts = [], []
        for kvh in range(2):
            k_lo, k_hi = (k2, k2r) if kvh == 0 else (k2r, k2)
            v_lo, v_hi = (v2, v2r) if kvh == 0 else (v2r, v2)
            k_half = (jnp.where(low, k_lo, 0.0).astype(BF16), jnp.where(low, 0.0, k_hi).astype(BF16))
            v_exts.append((jnp.concatenate([jnp.where(v_low, v_lo, 0.0), ones_lo], axis=1).astype(BF16),
                           jnp.concatenate([jnp.where(v_high, v_hi, 0.0), 1.0 - ones_lo], axis=1).astype(BF16)))
            col0 = kvh * n_slab * LANES
            qs = jnp.concatenate([q_ref[rows, col0 + p * LANES:col0 + (p + 1) * LANES] for p in range(n_slab)],
                                 axis=0)
            scores.append([lax.dot_general(qs, k_half[parity], DN_LAST, preferred_element_type=F32)
                           for parity in range(2)])

        probs = []
        for kvh in range(2):
            pe = []
            for parity in range(2):
                head0 = (2 * jj + kvh) * heads_per_kv + parity
                slabs = []
                for p in range(n_slab):
                    sp = scores[kvh][parity][p * WINDOW:(p + 1) * WINDOW, :] + bias
                    sp = jnp.concatenate([jnp.where(sink_col, sink_ref[head0 + 2 * p], sp[:, :LANES]),
                                          sp[:, LANES:]], axis=1)
                    slabs.append(jnp.exp(sp - jnp.max(sp, axis=-1, keepdims=True)).astype(BF16))
                pe.append(jnp.concatenate(slabs, axis=0))
            probs.append(pe)

        for kvh in range(2):
            res = [jnp.dot(probs[kvh][parity], v_exts[kvh][parity], preferred_element_type=F32)
                   for parity in range(2)]
            out = (res[0][:, :LANES] + res[1][:, :LANES]) / (res[0][:, LANES:] + res[1][:, LANES:])
            col0 = kvh * n_slab * LANES
            for p in range(n_slab):
                o_ref[rows, col0 + p * LANES:col0 + (p + 1) * LANES] = (
                    out[p * WINDOW:(p + 1) * WINDOW, :].astype(BF16))


def _attention(q, k, v, sinks):
    t, dq = q.shape
    qb = 4
    rows = qb * WINDOW
    gw = 2 * dq // ATT_KV_HEADS
    a_idx = jnp.arange(WINDOW)[:, None]
    c_idx = jnp.arange(2 * WINDOW)[None, :]
    band = (c_idx >= a_idx + 1) & (c_idx <= a_idx + WINDOW)
    bias = jnp.stack([jnp.where(band & (c_idx >= WINDOW), 0.0, -jnp.inf),
                      jnp.where(band, 0.0, -jnp.inf)]).astype(F32)

    def cur(n, j):
        return (n, j)

    def prev(n, j):
        return (jnp.maximum(qb * n - 1, 0), j)

    return pl.pallas_call(
        _attn_kernel,
        grid=(t // rows, ATT_KV_HEADS // 2),
        in_specs=[pl.BlockSpec(memory_space=pltpu.SMEM),
                  pl.BlockSpec((2, WINDOW, 2 * WINDOW), lambda n, j: (0, 0, 0)),
                  pl.BlockSpec((rows, gw), lambda n, j: (n, j)),
                  pl.BlockSpec((rows, LANES), cur),
                  pl.BlockSpec((WINDOW, LANES), prev),
                  pl.BlockSpec((rows, LANES), cur),
                  pl.BlockSpec((WINDOW, LANES), prev)],
        out_specs=pl.BlockSpec((rows, gw), lambda n, j: (n, j)),
        out_shape=jax.ShapeDtypeStruct((t, dq), BF16),
        compiler_params=_cparams("parallel", "parallel"),
        name="swa_attention",
    )(sinks, bias, q, k, k, v, v)


def kernel(x, p, positions, mixer_norm_g, hgrn_w_in, hgrn_w_out, hgrn_lb_logits, hgrn_out_norm_g,
           kv_norm_g, w_kv, attn_w_q, attn_w_o, attn_sinks, mlp_norm_g, mlp_w1, mlp_w2,
           ple_norm_g, ple_w_gate, ple_w_up, final_norm_g):
    batch, t, d = x.shape
    depth = mlp_w1.shape[0]
    n_a = hgrn_w_in.shape[0]
    assert batch == 1

    w_in, w_kvb, w_up = hgrn_w_in[0:1].astype(BF16), w_kv.astype(BF16)[None], ple_w_up.astype(BF16)
    mixer_w = [hgrn_w_out[0:1].astype(BF16)] if n_a > 0 else [attn_w_q[0:1].astype(BF16),
                                                              attn_w_o[0:1].astype(BF16)]
    if n_a == 0:
        w1, w2 = mlp_w1[0:1].astype(BF16), mlp_w2[0:1].astype(BF16)

    cos, sin = _rope_tables(positions)
    h = x.reshape(t, d)
    hb, ss = _prep_stream(h, mixer_norm_g[0])
    k_sh = v_sh = None

    for layer in range(depth):
        mlp_gain = mlp_norm_g[layer][None, :]
        if layer < n_a:
            jobs = [(mlp_w1, 0), (mlp_w2, 0)] if layer == 0 else []
            proj, cast = _proj(hb, ss, 0, w_in, 0, F32, "hgrn_in_proj", jobs)
            if layer == 0:
                w1, w2 = cast
            jobs = [(hgrn_w_in, layer + 1)] if layer + 1 < n_a else []
            mixed, cast = _hgrn_scan(proj, hgrn_lb_logits, hgrn_out_norm_g, layer, jobs)
            if jobs:
                w_in = cast[0]
            h, hb, ss = _resid_matmul(mixed, mixer_w[0], 0, h, mlp_gain, "hgrn_out_proj")
        else:
            j = layer - n_a
            q = _proj_rope(hb, ss, 0, mixer_w[0], 0, cos, sin, ATT_HEAD_DIM ** -0.5)
            mixed = _attention(q, k_sh, v_sh, attn_sinks[j])
            h, hb, ss = _resid_matmul(mixed, mixer_w[1], 0, h, mlp_gain, "attn_out_proj")

        jobs = [(ple_w_gate, layer)]
        if layer + 1 < depth:
            jobs += [(mlp_w1, layer + 1), (mlp_w2, layer + 1)]
            jobs += ([(hgrn_w_out, layer + 1)] if layer + 1 < n_a
                     else [(attn_w_q, layer + 1 - n_a), (attn_w_o, layer + 1 - n_a)])
        (h, hb, ss), cast = _mlp(hb, ss, w1, w2, 0, h, ple_norm_g[layer][None, :], jobs)
        w_gate = cast[0]
        if layer + 1 < depth:
            w1, w2, mixer_w = cast[1], cast[2], cast[3:]

        if layer + 1 < depth:
            next_gains = [mixer_norm_g[layer + 1]]
            if layer == n_a - 1:
                next_gains.append(kv_norm_g)
        else:
            next_gains = [final_norm_g]
        h, hb, ss = _ple(hb, ss, w_gate, 0, p, w_up, layer, h, jnp.stack(next_gains))

        if layer == n_a - 1:
            k_sh, v_sh = _kv_proj(hb, ss, 1, w_kvb, cos, sin)

    return _final_norm(h, ss, final_norm_g).reshape(batch, t, d)
```
